```python
import math
import jax, jax.numpy as jnp
from jax import lax
import numpy as np

D_MODEL = 1024
BATCH = 1
SEQ = 16384
DEPTH = 1
DEC_BATCH = 128
DEC_SEQ = 4
PAST_LEN = 8192
PAGE_SIZE = 128

HEAD_DIM = 64
SB_HEADS = (D_MODEL // 2) // HEAD_DIM
NSA_HEADS = (D_MODEL // 2) // HEAD_DIM
NSA_KV_HEADS = NSA_HEADS // 4
NSA_GROUP = NSA_HEADS // NSA_KV_HEADS
SB_WIDTH = SB_HEADS * HEAD_DIM
NSA_WIDTH = NSA_HEADS * HEAD_DIM
NSA_KV_WIDTH = NSA_KV_HEADS * HEAD_DIM
MIX_WIDTH = SB_WIDTH + NSA_WIDTH
CMP_BLOCK = 32
CMP_STRIDE = 16
CMP_HIDDEN = HEAD_DIM
SEL_BLOCK = 64
SEL_TOPK = 16
WINDOW = 512
Q_BLOCK = 128
NUM_BUCKETS = 32
MAX_EXACT = NUM_BUCKETS // 2
MAX_DISTANCE = 1024
D_FF = 256 * ((8 * D_MODEL // 3 + 255) // 256)
RMS_EPS = 1e-6
IN_SPLITS = (SB_WIDTH, SB_WIDTH, SB_WIDTH, NSA_WIDTH, NSA_KV_WIDTH, NSA_KV_WIDTH, NSA_KV_WIDTH, NSA_KV_WIDTH, NSA_KV_WIDTH, NSA_KV_WIDTH, 3 * NSA_HEADS)
IN_WIDTH = sum(IN_SPLITS)

kernel_name = 'hymba_stickbreak_nsa_macaron'


def rmsnorm(x, g):
    xf = x.astype(jnp.float32)
    y = xf * lax.rsqrt(jnp.mean(xf * xf, axis=-1, keepdims=True) + RMS_EPS)
    return (y * g.astype(jnp.float32)).astype(x.dtype)


def ffn_half(x, pre_g, post_g, wg, wu, wd):
    h = rmsnorm(x, pre_g)
    u = jax.nn.silu(h @ wg) * (h @ wu)
    return x + 0.5 * rmsnorm(u @ wd, post_g)


def t5_bucket(dist):
    n = jnp.maximum(dist, 0)
    nf = jnp.maximum(n, 1).astype(jnp.float32)
    large = MAX_EXACT + (jnp.log(nf / MAX_EXACT) / math.log(MAX_DISTANCE / MAX_EXACT) * (NUM_BUCKETS - MAX_EXACT)).astype(jnp.int32)
    large = jnp.minimum(large, NUM_BUCKETS - 1)
    return jnp.where(n < MAX_EXACT, n, large)


def masked_softmax(s, valid):
    s = jnp.where(valid, s, -jnp.inf)
    m = jnp.max(s, axis=-1, keepdims=True)
    m = jnp.where(jnp.isfinite(m), m, 0.0)
    e = jnp.where(valid, jnp.exp(s - m), 0.0)
    return e / jnp.maximum(e.sum(-1, keepdims=True), 1e-30)


def sb_weights(z, valid, logsurv):
    ln = jnp.where(valid, jax.nn.log_sigmoid(-z), 0.0)
    suffix = lax.cumsum(ln, axis=ln.ndim - 1, reverse=True) - ln
    a = jnp.where(valid, jnp.exp(jax.nn.log_sigmoid(z) + suffix + logsurv[..., None]), 0.0)
    return a, ln.sum(-1)


def sb_prompt(q, k, v):
    b_, s_ = q.shape[:2]
    scale = HEAD_DIM ** -0.5

    def q_block(bi):
        qs = bi * Q_BLOCK
        qb = lax.dynamic_slice_in_dim(q, qs, Q_BLOCK, axis=1)
        q_pos = qs + jnp.arange(Q_BLOCK)

        def body(i, carry):
            logsurv, acc = carry
            ks = (bi - i) * Q_BLOCK
            kb = lax.dynamic_slice_in_dim(k, ks, Q_BLOCK, axis=1)
            vb = lax.dynamic_slice_in_dim(v, ks, Q_BLOCK, axis=1)
            k_pos = ks + jnp.arange(Q_BLOCK)
            z = jnp.einsum('bqhd,bkhd->bhqk', qb, kb, preferred_element_type=jnp.float32) * scale
            a, ln_sum = sb_weights(z, k_pos[None, :] < q_pos[:, None], logsurv)
            return logsurv + ln_sum, acc + jnp.einsum('bhqk,bkhd->bhqd', a, vb.astype(jnp.float32))

        init = (jnp.zeros((b_, SB_HEADS, Q_BLOCK), jnp.float32),
                jnp.zeros((b_, SB_HEADS, Q_BLOCK, HEAD_DIM), jnp.float32))
        _, acc = lax.fori_loop(0, bi + 1, body, init)
        return acc.transpose(0, 2, 1, 3)

    out = lax.map(q_block, jnp.arange(s_ // Q_BLOCK))
    return out.transpose(1, 0, 2, 3, 4).reshape(b_, s_, SB_WIDTH)


def sb_sample(q, k_past, v_past, k_new, v_new):
    b_, t_ = q.shape[:2]
    past = k_past.shape[1]
    scale = HEAD_DIM ** -0.5
    z = jnp.concatenate([
        jnp.einsum('bqhd,bkhd->bhqk', q, k_past, preferred_element_type=jnp.float32),
        jnp.einsum('bqhd,bkhd->bhqk', q, k_new, preferred_element_type=jnp.float32)], axis=-1) * scale
    q_pos = past + jnp.arange(t_)
    k_pos = jnp.arange(past + t_)
    a, _ = sb_weights(z, k_pos[None, :] < q_pos[:, None], jnp.zeros(z.shape[:-1], jnp.float32))
    o = (jnp.einsum('bhqk,bkhd->bqhd', a[..., :past], v_past.astype(jnp.float32))
         + jnp.einsum('bhqk,bkhd->bqhd', a[..., past:], v_new.astype(jnp.float32)))
    return o.reshape(b_, t_, SB_WIDTH)


def compress(rows, pos, w1, w2):
    b_, l_ = rows.shape[:2]
    r = rows.reshape(b_, l_ // CMP_STRIDE, CMP_STRIDE, NSA_KV_HEADS, HEAD_DIM)
    h = (jnp.einsum('bnlkd,lde->bnke', r[:, :-1], w1[:CMP_STRIDE])
         + jnp.einsum('bnlkd,lde->bnke', r[:, 1:], w1[CMP_STRIDE:])
         + jnp.einsum('ld,lde->e', pos, w1))
    return jnp.einsum('bnke,ef->bnkf', jax.nn.silu(h), w2)


def nsa_core(q, q_pos, ck, cv, sk_blk, sv_blk, wk, wv, w_pos, gates, rel_bias):
    b_, t_ = q.shape[:2]
    n_cmp = ck.shape[1]
    n_sel = sk_blk.shape[1]
    per = SEL_BLOCK // CMP_STRIDE
    scale = HEAD_DIM ** -0.5
    rb = rel_bias.astype(jnp.float32).reshape(NUM_BUCKETS, NSA_KV_HEADS, NSA_GROUP)

    c_end = CMP_STRIDE * jnp.arange(n_cmp) + CMP_BLOCK - 1
    dist_c = q_pos[:, None] - c_end[None, :]
    s_c = (jnp.einsum('btkgd,bnkd->bkgtn', q, ck, preferred_element_type=jnp.float32) * scale
           + rb[t5_bucket(dist_c)].transpose(2, 3, 0, 1))
    p_c = masked_softmax(s_c, dist_c >= 0)
    o_c = jnp.einsum('bkgtn,bnkd->btkgd', p_c, cv.astype(jnp.float32))

    imp = jnp.pad(p_c.sum(2), ((0, 0), (0, 0), (0, 0), (1, max(0, per * n_sel - n_cmp))))[..., :per * n_sel + 1]
    imp_s = imp[..., :per * n_sel].reshape(b_, NSA_KV_HEADS, t_, n_sel, per).sum(-1) + imp[..., per::per]
    blk = jnp.arange(n_sel)
    cur = (q_pos // SEL_BLOCK)[:, None]
    valid_s = blk[None, :] <= cur
    forced = (blk[None, :] == 0) | (blk[None, :] == cur) | (blk[None, :] == cur - 1)
    score = jnp.where(valid_s, jnp.where(forced, jnp.inf, imp_s), -jnp.inf)
    n_top = min(SEL_TOPK, n_sel)
    _, idx = lax.top_k(score, n_top)
    sel_ok = idx <= cur

    flat = idx.reshape(b_, NSA_KV_HEADS, t_ * n_top)
    bi = jnp.arange(b_)[:, None, None]
    hi = jnp.arange(NSA_KV_HEADS)[None, :, None]
    n_keys = n_top * SEL_BLOCK
    ks = sk_blk[bi, flat, :, hi, :].reshape(b_, NSA_KV_HEADS, t_, n_keys, HEAD_DIM)
    vs = sv_blk[bi, flat, :, hi, :].reshape(b_, NSA_KV_HEADS, t_, n_keys, HEAD_DIM)
    k_pos = (idx[..., None] * SEL_BLOCK + jnp.arange(SEL_BLOCK)).reshape(b_, NSA_KV_HEADS, t_, n_keys)
    dist_s = q_pos[:, None] - k_pos
    valid_k = (dist_s >= 0) & jnp.repeat(sel_ok, SEL_BLOCK, axis=-1)
    bias_s = jnp.moveaxis(rb[t5_bucket(dist_s), jnp.arange(NSA_KV_HEADS)[None, :, None, None]], -1, 2)
    s_s = jnp.einsum('btkgd,bktld->bkgtl', q, ks, preferred_element_type=jnp.float32) * scale + bias_s
    p_s = masked_softmax(s_s, valid_k[:, :, None])
    o_s = jnp.einsum('bkgtl,bktld->btkgd', p_s, vs.astype(jnp.float32))

    dist_w = q_pos[:, None] - w_pos[None, :]
    valid_w = (dist_w >= 0) & (dist_w < WINDOW) & (w_pos[None, :] >= 0)
    s_w = (jnp.einsum('btkgd,bwkd->bkgtw', q, wk, preferred_element_type=jnp.float32) * scale
           + rb[t5_bucket(dist_w)].transpose(2, 3, 0, 1))
    p_w = masked_softmax(s_w, valid_w)
    o_w = jnp.einsum('bkgtw,bwkd->btkgd', p_w, wv.astype(jnp.float32))

    g = jax.nn.sigmoid(gates.astype(jnp.float32))[..., None]
    o = g[:, :, 0] * o_c + g[:, :, 1] * o_s + g[:, :, 2] * o_w
    return o.reshape(b_, t_, NSA_WIDTH)


def nsa_prompt(q, ck, cv, sk_blk, sv_blk, wk, wv, gates, rel_bias):
    b_, s_ = q.shape[:2]
    pad = ((0, 0), (WINDOW, 0), (0, 0), (0, 0))
    wk_p = jnp.pad(wk, pad)
    wv_p = jnp.pad(wv, pad)

    def q_block(bi):
        qs = bi * Q_BLOCK
        band = WINDOW + Q_BLOCK
        return nsa_core(lax.dynamic_slice_in_dim(q, qs, Q_BLOCK, axis=1), qs + jnp.arange(Q_BLOCK),
                        ck, cv, sk_blk, sv_blk,
                        lax.dynamic_slice_in_dim(wk_p, qs, band, axis=1),
                        lax.dynamic_slice_in_dim(wv_p, qs, band, axis=1),
                        qs - WINDOW + jnp.arange(band),
                        lax.dynamic_slice_in_dim(gates, qs, Q_BLOCK, axis=1), rel_bias)

    out = lax.map(q_block, jnp.arange(s_ // Q_BLOCK))
    return out.transpose(1, 0, 2, 3).reshape(b_, s_, NSA_WIDTH)


def in_proj(x, pre_g, w_in):
    b_, t_ = x.shape[:2]
    p = rmsnorm(x, pre_g) @ w_in
    cuts = [int(c) for c in np.cumsum(IN_SPLITS)[:-1]]
    sq, sk, sv, nq, ck, cv, sk2, sv2, wk, wv, gt = jnp.split(p, cuts, axis=-1)
    sb = lambda a: a.reshape(b_, t_, SB_HEADS, HEAD_DIM)
    kv = lambda a: a.reshape(b_, t_, NSA_KV_HEADS, HEAD_DIM)
    return (sb(sq), sb(sk), sb(sv), nq.reshape(b_, t_, NSA_KV_HEADS, NSA_GROUP, HEAD_DIM),
            kv(ck), kv(cv), kv(sk2), kv(sv2), kv(wk), kv(wv),
            gt.reshape(b_, t_, 3, NSA_KV_HEADS, NSA_GROUP))


def out_proj(x, o_sb, o_nsa, sb_g, nsa_g, w_o, post_g):
    o = jnp.concatenate([rmsnorm(o_sb, sb_g), rmsnorm(o_nsa, nsa_g)], axis=-1).astype(x.dtype)
    return x + rmsnorm(o @ w_o, post_g)


def gather_pages(cache_l, page_table):
    n_seq, n_pages = page_table.shape
    return cache_l[page_table].reshape((n_seq, n_pages * PAGE_SIZE) + cache_l.shape[2:])


def pad_rows(rows, multiple):
    extra = (-rows.shape[1]) % multiple
    return jnp.pad(rows, ((0, 0), (0, extra), (0, 0), (0, 0)))


def to_blocks(rows):
    b_, l_ = rows.shape[:2]
    return rows.reshape((b_, l_ // SEL_BLOCK, SEL_BLOCK) + rows.shape[2:])


def setup_inputs(seed: int = 0) -> dict:
    key = jax.random.key(seed)
    keys = list(jax.random.split(key, 40))

    def nrm(shape, scale):
        return jax.random.normal(keys.pop(), shape, jnp.float32) * scale

    def gain(shape):
        return 1.0 + nrm(shape, 0.05)

    n_pages = PAST_LEN // PAGE_SIZE
    n_used = DEC_BATCH * n_pages
    n_pool = n_used + (n_used + 3) // 4
    w_buf = min(WINDOW, PAST_LEN)
    page_table = jax.random.permutation(keys.pop(), n_pool)[:n_used].reshape(DEC_BATCH, n_pages).astype(jnp.int32)
    sb_page = (DEPTH, n_pool, PAGE_SIZE, SB_HEADS, HEAD_DIM)
    kv_page = (DEPTH, n_pool, PAGE_SIZE, NSA_KV_HEADS, HEAD_DIM)
    win = (DEPTH, DEC_BATCH, w_buf, NSA_KV_HEADS, HEAD_DIM)
    d, f = D_MODEL, D_FF
    return {
        'x_prompt': nrm((BATCH, SEQ, d), 1.0),
        'x_sample': nrm((DEC_BATCH, DEC_SEQ, d), 1.0),
        'cache_sb_k': nrm(sb_page, 1.0),
        'cache_sb_v': nrm(sb_page, 1.0),
        'cache_cmp_k': nrm(kv_page, 1.0),
        'cache_cmp_v': nrm(kv_page, 1.0),
        'cache_sel_k': nrm(kv_page, 1.0),
        'cache_sel_v': nrm(kv_page, 1.0),
        'state_win_k': nrm(win, 1.0),
        'state_win_v': nrm(win, 1.0),
        'page_table': page_table,
        'ffn1_pre_g': gain((DEPTH, d)),
        'ffn1_post_g': gain((DEPTH, d)),
        'ffn1_w_gate': nrm((DEPTH, d, f), d ** -0.5),
        'ffn1_w_up': nrm((DEPTH, d, f), d ** -0.5),
        'ffn1_w_down': nrm((DEPTH, f, d), f ** -0.5),
        'mix_pre_g': gain((DEPTH, d)),
        'mix_post_g': gain((DEPTH, d)),
        'w_in': nrm((DEPTH, d, IN_WIDTH), d ** -0.5),
        'cmp_pos_k': nrm((DEPTH, CMP_BLOCK, HEAD_DIM), 0.1),
        'cmp_pos_v': nrm((DEPTH, CMP_BLOCK, HEAD_DIM), 0.1),
        'w_cmp_k1': nrm((DEPTH, CMP_BLOCK, HEAD_DIM, CMP_HIDDEN), (CMP_BLOCK * HEAD_DIM) ** -0.5),
        'w_cmp_k2': nrm((DEPTH, CMP_HIDDEN, HEAD_DIM), CMP_HIDDEN ** -0.5),
        'w_cmp_v1': nrm((DEPTH, CMP_BLOCK, HEAD_DIM, CMP_HIDDEN), (CMP_BLOCK * HEAD_DIM) ** -0.5),
        'w_cmp_v2': nrm((DEPTH, CMP_HIDDEN, HEAD_DIM), CMP_HIDDEN ** -0.5),
        'sb_out_g': gain((DEPTH, SB_WIDTH)),
        'nsa_out_g': gain((DEPTH, NSA_WIDTH)),
        'w_o': nrm((DEPTH, MIX_WIDTH, d), MIX_WIDTH ** -0.5),
        'rel_bias': nrm((NUM_BUCKETS, NSA_HEADS), 0.5),
        'ffn2_pre_g': gain((DEPTH, d)),
        'ffn2_post_g': gain((DEPTH, d)),
        'ffn2_w_gate': nrm((DEPTH, d, f), d ** -0.5),
        'ffn2_w_up': nrm((DEPTH, d, f), d ** -0.5),
        'ffn2_w_down': nrm((DEPTH, f, d), f ** -0.5),
    }


def reference(x_prompt, x_sample, cache_sb_k, cache_sb_v, cache_cmp_k, cache_cmp_v, cache_sel_k, cache_sel_v,
              state_win_k, state_win_v, page_table,
              ffn1_pre_g, ffn1_post_g, ffn1_w_gate, ffn1_w_up, ffn1_w_down,
              mix_pre_g, mix_post_g, w_in, cmp_pos_k, cmp_pos_v, w_cmp_k1, w_cmp_k2, w_cmp_v1, w_cmp_v2,
              sb_out_g, nsa_out_g, w_o, rel_bias,
              ffn2_pre_g, ffn2_post_g, ffn2_w_gate, ffn2_w_up, ffn2_w_down):
    s_ = x_prompt.shape[1]
    t_ = x_sample.shape[1]
    past = page_table.shape[1] * PAGE_SIZE
    w_keep = min(WINDOW, s_)
    w_buf = state_win_k.shape[2]
    newp = [[] for _ in range(8)]
    news = [[] for _ in range(8)]
    xp, xs = x_prompt, x_sample
    for l in range(DEPTH):
        xp = ffn_half(xp, ffn1_pre_g[l], ffn1_post_g[l], ffn1_w_gate[l], ffn1_w_up[l], ffn1_w_down[l])
        xs = ffn_half(xs, ffn1_pre_g[l], ffn1_post_g[l], ffn1_w_gate[l], ffn1_w_up[l], ffn1_w_down[l])

        sq, sk, sv, nq, ckr, cvr, skr, svr, wkr, wvr, gt = in_proj(xp, mix_pre_g[l], w_in[l])
        o_sb = sb_prompt(sq, sk, sv)
        ck = compress(ckr, cmp_pos_k[l], w_cmp_k1[l], w_cmp_k2[l])
        cv = compress(cvr, cmp_pos_v[l], w_cmp_v1[l], w_cmp_v2[l])
        o_nsa = nsa_prompt(nq, ck, cv, to_blocks(skr), to_blocks(svr), wkr, wvr, gt, rel_bias)
        xp = out_proj(xp, o_sb, o_nsa, sb_out_g[l], nsa_out_g[l], w_o[l], mix_post_g[l])
        for lst, a in zip(newp, (sk, sv, ckr, cvr, skr, svr, wkr[:, s_ - w_keep:], wvr[:, s_ - w_keep:])):
            lst.append(a)

        sq, sk, sv, nq, ckr, cvr, skr, svr, wkr, wvr, gt = in_proj(xs, mix_pre_g[l], w_in[l])
        o_sb = sb_sample(sq, gather_pages(cache_sb_k[l], page_table), gather_pages(cache_sb_v[l], page_table), sk, sv)
        ck = compress(pad_rows(jnp.concatenate([gather_pages(cache_cmp_k[l], page_table), ckr], axis=1), CMP_STRIDE),
                      cmp_pos_k[l], w_cmp_k1[l], w_cmp_k2[l])
        cv = compress(pad_rows(jnp.concatenate([gather_pages(cache_cmp_v[l], page_table), cvr], axis=1), CMP_STRIDE),
                      cmp_pos_v[l], w_cmp_v1[l], w_cmp_v2[l])
        sbk = to_blocks(pad_rows(jnp.concatenate([gather_pages(cache_sel_k[l], page_table), skr], axis=1), SEL_BLOCK))
        sbv = to_blocks(pad_rows(jnp.concatenate([gather_pages(cache_sel_v[l], page_table), svr], axis=1), SEL_BLOCK))
        wk = jnp.concatenate([state_win_k[l], wkr], axis=1)
        wv = jnp.concatenate([state_win_v[l], wvr], axis=1)
        w_pos = past - w_buf + jnp.arange(w_buf + t_)
        o_nsa = nsa_core(nq, past + jnp.arange(t_), ck, cv, sbk, sbv, wk, wv, w_pos, gt, rel_bias)
        xs = out_proj(xs, o_sb, o_nsa, sb_out_g[l], nsa_out_g[l], w_o[l], mix_post_g[l])
        for lst, a in zip(news, (sk, sv, ckr, cvr, skr, svr, wk[:, t_:], wv[:, t_:])):
            lst.append(a)

        xp = ffn_half(xp, ffn2_pre_g[l], ffn2_post_g[l], ffn2_w_gate[l], ffn2_w_up[l], ffn2_w_down[l])
        xs = ffn_half(xs, ffn2_pre_g[l], ffn2_post_g[l], ffn2_w_gate[l], ffn2_w_up[l], ffn2_w_down[l])

    stk = lambda lst: jnp.stack(lst, axis=0)
    return (xp, xs,
            stk(newp[0]), stk(newp[1]), stk(newp[2]), stk(newp[3]),
            stk(newp[4]), stk(newp[5]), stk(newp[6]), stk(newp[7]),
            stk(news[0]), stk(news[1]), stk(news[2]), stk(news[3]),
            stk(news[4]), stk(news[5]), stk(news[6]), stk(news[7]))
```

```python
import functools
import math

import numpy as np
import jax
import jax.numpy as jnp
from jax import lax
from jax.experimental import pallas as pl
from jax.experimental.pallas import tpu as pltpu

F32 = jnp.float32
BF16 = jnp.bfloat16

HEAD_DIM = 64
SB_HEADS = 8
NSA_HEADS = 8
NSA_KV_HEADS = 2
NSA_GROUP = 4
CMP_BLOCK = 32
CMP_STRIDE = 16
SEL_BLOCK = 64
SEL_TOPK = 16
SEL_PER_CMP = SEL_BLOCK // CMP_STRIDE
WINDOW = 512
Q_BLOCK = 128
NUM_BUCKETS = 32
MAX_EXACT = 16
MAX_DISTANCE = 1024
RMS_EPS = 1e-6
PAGE_SIZE = 128
SCALE = HEAD_DIM ** -0.5

LANES = 128
VMEM_LIMIT = 56 * 1024 * 1024
NEG = -1e30
EXP_ZERO = -104.0


def _bucket_thresholds():
    n = np.arange(0, 4 * MAX_DISTANCE)
    nf = np.maximum(n, 1).astype(np.float64)
    large = MAX_EXACT + (np.log(nf / MAX_EXACT) / math.log(MAX_DISTANCE / MAX_EXACT)
                         * (NUM_BUCKETS - MAX_EXACT)).astype(np.int64)
    bucket = np.where(n < MAX_EXACT, n, np.minimum(large, NUM_BUCKETS - 1))
    return [int(np.argmax(bucket >= b)) for b in range(1, NUM_BUCKETS)]


BUCKET_THR = _bucket_thresholds()
FAR_DIST = BUCKET_THR[-1]


def _cparams(sem):
    return pltpu.CompilerParams(dimension_semantics=sem, vmem_limit_bytes=VMEM_LIMIT)


def _dot(a, b):
    return jnp.dot(a, b, preferred_element_type=F32)


def _dot_nt(a, b):
    return lax.dot_general(a, b, (((1,), (1,)), ((), ())), preferred_element_type=F32)


def _dot_tn(a, b):
    return lax.dot_general(a, b, (((0,), (0,)), ((), ())), preferred_element_type=F32)


def _split_dot(x, w):
    hi = x.astype(BF16)
    lo = (x - hi.astype(F32)).astype(BF16)
    return _dot(hi, w) + _dot(lo, w)


def _split_dot_left(w, x):
    hi = x.astype(BF16)
    lo = (x - hi.astype(F32)).astype(BF16)
    return _dot(w, hi) + _dot(w, lo)


def _rms(x, g):
    return x * lax.rsqrt(jnp.mean(x * x, axis=-1, keepdims=True) + RMS_EPS) * g


def _ffn_kernel(x_ref, pre_ref, post_ref, wg_ref, wu_ref, wd_ref, o_ref, acc_ref, *, fc):
    x = x_ref[...]
    hb = _rms(x, pre_ref[...]).astype(BF16)
    n_f = wg_ref.shape[1]
    for c in range(n_f // fc):
        g = _dot(hb, wg_ref[:, c * fc:(c + 1) * fc])
        u = _dot(hb, wu_ref[:, c * fc:(c + 1) * fc])
        a = (jax.nn.silu(g) * u).astype(BF16)
        part = _dot(a, wd_ref[c * fc:(c + 1) * fc, :])
        if c == 0:
            acc_ref[...] = part
        else:
            acc_ref[...] += part
    o_ref[...] = x + 0.5 * _rms(acc_ref[...], post_ref[...])


def _ffn(x, pre_g, post_g, wg, wu, wd, *, tm):
    t, d = x.shape
    f = wg.shape[1]
    fc = 256
    const = lambda i: (0, 0)
    return pl.pallas_call(
        functools.partial(_ffn_kernel, fc=fc),
        grid=(t // tm,),
        in_specs=[
            pl.BlockSpec((tm, d), lambda i: (i, 0)),
            pl.BlockSpec((1, d), const),
            pl.BlockSpec((1, d), const),
            pl.BlockSpec((d, f), const, pipeline_mode=pl.Buffered(1)),
            pl.BlockSpec((d, f), const, pipeline_mode=pl.Buffered(1)),
            pl.BlockSpec((f, d), const, pipeline_mode=pl.Buffered(1)),
        ],
        out_specs=pl.BlockSpec((tm, d), lambda i: (i, 0)),
        out_shape=jax.ShapeDtypeStruct((t, d), F32),
        scratch_shapes=[pltpu.VMEM((tm, d), F32)],
        compiler_params=_cparams(("parallel",)),
        name="ffn_half",
    )(x, pre_g.reshape(1, d), post_g.reshape(1, d), wg, wu, wd)


SB_W = SB_HEADS * HEAD_DIM
NSA_W = NSA_HEADS * HEAD_DIM
KV_W = NSA_KV_HEADS * HEAD_DIM
GATE_W = 3 * NSA_HEADS
IN_PIECES = (("sq", SB_W), ("sk", SB_W), ("sv", SB_W), ("nq", NSA_W),
             ("ck", KV_W), ("cv", KV_W), ("sk2", KV_W), ("sv2", KV_W),
             ("wk", KV_W), ("wv", KV_W), ("gt", LANES))
IN_BF16 = ("sq", "sk", "sv", "nq", "sk2", "sv2", "wk", "wv")


def _in_proj_kernel(x_ref, g_ref, w_ref, *out_refs):
    hb = _rms(x_ref[...], g_ref[...]).astype(BF16)
    f32_refs = out_refs[:len(IN_PIECES)]
    bf_refs = dict(zip(IN_BF16, out_refs[len(IN_PIECES):]))
    off = 0
    for (name, width), o_ref in zip(IN_PIECES, f32_refs):
        p = _dot(hb, w_ref[:, off:off + width])
        o_ref[...] = p
        if name in bf_refs:
            bf_refs[name][...] = p.astype(BF16)
        off += width


def _in_proj(x, pre_g, w_in_pad, *, tm):
    t, d = x.shape
    wp = w_in_pad.shape[1]
    row = lambda i: (i, 0)
    const = lambda i: (0, 0)
    widths = dict(IN_PIECES)
    out_shape = ([jax.ShapeDtypeStruct((t, w), F32) for _, w in IN_PIECES]
                 + [jax.ShapeDtypeStruct((t, widths[n]), BF16) for n in IN_BF16])
    out_specs = ([pl.BlockSpec((tm, w), row) for _, w in IN_PIECES]
                 + [pl.BlockSpec((tm, widths[n]), row) for n in IN_BF16])
    outs = pl.pallas_call(
        _in_proj_kernel,
        grid=(t // tm,),
        in_specs=[pl.BlockSpec((tm, d), row),
                  pl.BlockSpec((1, d), const),
                  pl.BlockSpec((d, wp), const, pipeline_mode=pl.Buffered(1))],
        out_specs=out_specs,
        out_shape=out_shape,
        compiler_params=_cparams(("parallel",)),
        name="in_proj",
    )(x, pre_g.reshape(1, d), w_in_pad)
    f32 = {n: o for (n, _), o in zip(IN_PIECES, outs[:len(IN_PIECES)])}
    bf = dict(zip(IN_BF16, outs[len(IN_PIECES):]))
    return f32, bf


def _out_proj_kernel(x_ref, osb_ref, onsa_ref, sbg_ref, nsag_ref, wo_ref, post_ref, o_ref):
    a = _rms(osb_ref[...], sbg_ref[...]).astype(BF16)
    b = _rms(onsa_ref[...], nsag_ref[...]).astype(BF16)
    y = _dot(a, wo_ref[:SB_W, :]) + _dot(b, wo_ref[SB_W:, :])
    o_ref[...] = x_ref[...] + _rms(y, post_ref[...])


def _out_proj(x, o_sb, o_nsa, sb_g, nsa_g, w_o, post_g, *, tm):
    t, d = x.shape
    row = lambda i: (i, 0)
    const = lambda i: (0, 0)
    return pl.pallas_call(
        _out_proj_kernel,
        grid=(t // tm,),
        in_specs=[pl.BlockSpec((tm, d), row),
                  pl.BlockSpec((tm, SB_W), row),
                  pl.BlockSpec((tm, NSA_W), row),
                  pl.BlockSpec((1, SB_W), const),
                  pl.BlockSpec((1, NSA_W), const),
                  pl.BlockSpec((SB_W + NSA_W, d), const, pipeline_mode=pl.Buffered(1)),
                  pl.BlockSpec((1, d), const)],
        out_specs=pl.BlockSpec((tm, d), row),
        out_shape=jax.ShapeDtypeStruct((t, d), F32),
        compiler_params=_cparams(("parallel",)),
        name="out_proj",
    )(x, o_sb, o_nsa, sb_g.reshape(1, SB_W), nsa_g.reshape(1, NSA_W), w_o, post_g.reshape(1, d))


def _sb_tri():
    r = lax.broadcasted_iota(jnp.int32, (Q_BLOCK, 2 * Q_BLOCK), 0)
    c = lax.broadcasted_iota(jnp.int32, (Q_BLOCK, 2 * Q_BLOCK), 1)
    return jnp.where((c >= Q_BLOCK) | (r > c), 1.0, 0.0).astype(BF16)


def _sb_prompt_kernel(q_ref, k_ref, v_ref, o_ref, ls_ref, acc_ref):
    i = pl.program_id(1)
    tri = _sb_tri()
    r = lax.broadcasted_iota(jnp.int32, (Q_BLOCK, Q_BLOCK), 0)
    c = lax.broadcasted_iota(jnp.int32, (Q_BLOCK, Q_BLOCK), 1)
    strictly_earlier = c < r

    for hh in range(LANES // HEAD_DIM):
        lanes = slice(hh * HEAD_DIM, (hh + 1) * HEAD_DIM)
        q = q_ref[:, lanes]

        def block(j, valid):
            ks = pl.multiple_of(j * Q_BLOCK, Q_BLOCK)
            kb = k_ref[pl.ds(ks, Q_BLOCK), lanes]
            vb = v_ref[pl.ds(ks, Q_BLOCK), lanes]
            z = _dot_nt(q, kb) * SCALE
            ln = jax.nn.log_sigmoid(-z)
            ln_m = ln if valid is None else jnp.where(valid, ln, 0.0)
            st = _split_dot(ln_m, tri)
            a = jnp.exp(z + ln + st[:, :Q_BLOCK] + ls_ref[...])
            if valid is not None:
                a = jnp.where(valid, a, 0.0)
            acc_ref[...] += _dot(a.astype(BF16), vb)
            ls_ref[...] += st[:, Q_BLOCK:]
            return jnp.max(ls_ref[...])

        ls_ref[...] = jnp.zeros_like(ls_ref)
        acc_ref[...] = jnp.zeros_like(acc_ref)
        top = block(i, strictly_earlier)

        def cond(carry):
            j, top = carry
            return (j >= 0) & (top > EXP_ZERO)

        def body(carry):
            j, _ = carry
            return j - 1, block(j, None)

        lax.while_loop(cond, body, (i - 1, top))
        o_ref[:, lanes] = acc_ref[...]


def _sb_prompt(q, k, v):
    t = q.shape[0]
    return pl.pallas_call(
        _sb_prompt_kernel,
        grid=(SB_W // LANES, t // Q_BLOCK),
        in_specs=[pl.BlockSpec((Q_BLOCK, LANES), lambda h, i: (i, h)),
                  pl.BlockSpec((t, LANES), lambda h, i: (0, h)),
                  pl.BlockSpec((t, LANES), lambda h, i: (0, h))],
        out_specs=pl.BlockSpec((Q_BLOCK, LANES), lambda h, i: (i, h)),
        out_shape=jax.ShapeDtypeStruct((t, SB_W), F32),
        scratch_shapes=[pltpu.VMEM((Q_BLOCK, Q_BLOCK), F32),
                        pltpu.VMEM((Q_BLOCK, HEAD_DIM), F32)],
        compiler_params=_cparams(("parallel", "parallel")),
        name="sb_prompt",
    )(q, k, v)


def _pad_w_in(w_in):
    wp = sum(w for _, w in IN_PIECES)
    return jnp.pad(w_in, ((0, 0), (0, wp - w_in.shape[1]))).astype(BF16)


def _bias_from_dist(dist, rb_at):
    out = jnp.where(dist >= BUCKET_THR[0], rb_at(1), rb_at(0))
    for b in range(2, NUM_BUCKETS):
        out = jnp.where(dist >= BUCKET_THR[b - 1], rb_at(b), out)
    return out


CMP_CHUNK = 128
CMP_CHUNK_SPAN = CMP_CHUNK * CMP_STRIDE
QB_PER_CHUNK = CMP_CHUNK_SPAN // Q_BLOCK
NEAR_TILES = -(-(FAR_DIST + Q_BLOCK - 1) // Q_BLOCK)


def _bias_prompt_kernel(rb_ref, bt_ref, bct_ref):
    h = pl.program_id(0)
    row = lax.broadcasted_iota(jnp.int32, (Q_BLOCK, Q_BLOCK), 0)
    col = lax.broadcasted_iota(jnp.int32, (Q_BLOCK, Q_BLOCK), 1)
    rb_at = lambda b: rb_ref[b, h]

    def tile(d, carry):
        bt_ref[d] = _bias_from_dist(Q_BLOCK * d + col - row, rb_at)
        return carry

    lax.fori_loop(0, NEAR_TILES, tile, 0)

    def ctile(rw, carry):
        r = rw // 2
        w = rw % 2
        dist = (Q_BLOCK * r + col - CMP_STRIDE * row - (CMP_BLOCK - 1)
                + CMP_CHUNK_SPAN * (1 - w))
        bct_ref[r, w] = _bias_from_dist(dist, rb_at)
        return carry

    lax.fori_loop(0, 2 * QB_PER_CHUNK, ctile, 0)


def _bias_prompt(rel_bias):
    nh = rel_bias.shape[1]
    return pl.pallas_call(
        _bias_prompt_kernel,
        grid=(nh,),
        in_specs=[pl.BlockSpec(memory_space=pltpu.SMEM)],
        out_specs=[pl.BlockSpec((None, NEAR_TILES, Q_BLOCK, Q_BLOCK), lambda h: (h, 0, 0, 0)),
                   pl.BlockSpec((None, QB_PER_CHUNK, 2, CMP_CHUNK, Q_BLOCK), lambda h: (h, 0, 0, 0, 0))],
        out_shape=[jax.ShapeDtypeStruct((nh, NEAR_TILES, Q_BLOCK, Q_BLOCK), F32),
                   jax.ShapeDtypeStruct((nh, QB_PER_CHUNK, 2, CMP_CHUNK, Q_BLOCK), F32)],
        compiler_params=_cparams(("parallel",)),
        name="bias_prompt",
    )(rel_bias)


GROUP_W = CMP_STRIDE * KV_W


def _compress_weights(w1, w2, pos):
    eye = jnp.eye(NSA_KV_HEADS, dtype=w1.dtype)
    halves = []
    for half in range(2):
        wh = w1[half * CMP_STRIDE:(half + 1) * CMP_STRIDE]
        halves.append(jnp.einsum("lde,pk->lpdke", wh, eye).reshape(GROUP_W, KV_W))
    w_big = jnp.concatenate(halves, axis=1).astype(BF16)
    w_pos = jnp.tile(w1.reshape(CMP_BLOCK * HEAD_DIM, HEAD_DIM), (1, NSA_KV_HEADS)).astype(BF16)
    pos_row = jnp.pad(pos.reshape(1, CMP_BLOCK * HEAD_DIM), ((0, 7), (0, 0))).astype(BF16)
    w2_bd = jnp.einsum("ef,pk->pekf", w2, eye).reshape(KV_W, KV_W).astype(BF16)
    return w_big, w_pos, pos_row, w2_bd


def _compress_finish(ab_ref, n_rows, pos_row, w_pos, w2_bd):
    c = _dot(pos_row, w_pos)[0:1, :]
    h = ab_ref[0:n_rows, 0:KV_W] + ab_ref[1:n_rows + 1, KV_W:2 * KV_W] + c
    return _dot(jax.nn.silu(h).astype(BF16), w2_bd)


def _compress_kernel(g_ref, wbig_ref, wpos_ref, pos_ref, w2_ref, o_ref, ab_ref):
    n = g_ref.shape[0]
    ab_ref[0:n, :] = _dot(g_ref[...].astype(BF16), wbig_ref[...])
    ab_ref[n:n + 8, :] = jnp.zeros((8, 2 * KV_W), F32)
    o_ref[...] = _compress_finish(ab_ref, n, pos_ref[...], wpos_ref[...], w2_ref[...])


def _compress_prompt(rows, w_big, w_pos, pos_row, w2_bd):
    g = rows.reshape(rows.shape[0] // CMP_STRIDE, GROUP_W)
    n = g.shape[0]
    return pl.pallas_call(
        _compress_kernel,
        out_shape=jax.ShapeDtypeStruct((n, KV_W), F32),
        scratch_shapes=[pltpu.VMEM((n + 8, 2 * KV_W), F32)],
        compiler_params=pltpu.CompilerParams(vmem_limit_bytes=VMEM_LIMIT),
        name="compress_prompt",
    )(g, w_big, w_pos, pos_row, w2_bd)


BIG = 3e38
GQ = NSA_GROUP * Q_BLOCK


def _importance_matrices(n_half, n_cmp):
    r = np.arange(n_half)[:, None]
    n = np.arange(n_cmp)[None, :]
    mats = []
    for parity in range(2):
        j = 2 * r + parity
        mats.append(((n >= SEL_PER_CMP * j - 1) & (n <= SEL_PER_CMP * j + SEL_PER_CMP - 1)).astype(np.float32))
    return jnp.asarray(mats[0], BF16), jnp.asarray(mats[1], BF16)


def _topk_mask(imp_e, imp_o, cur, n_top):
    shape = imp_e.shape
    jr = lax.broadcasted_iota(jnp.int32, shape, 0)
    je = (2 * jr).astype(F32)
    jo = (2 * jr + 1).astype(F32)
    curf = cur.astype(F32)

    def encode(imp, j):
        valid = j <= curf
        forced = (j == 0.0) | (j == curf) | (j == curf - 1.0)
        return jnp.where(valid, jnp.where(forced, BIG, imp), -1.0), valid

    se, valid_e = encode(imp_e, je)
    so, valid_o = encode(imp_o, jo)
    sel_e = jnp.zeros(shape, F32)
    sel_o = jnp.zeros(shape, F32)
    for _ in range(n_top):
        mx = jnp.maximum(jnp.max(se, axis=0, keepdims=True), jnp.max(so, axis=0, keepdims=True))
        cand = jnp.minimum(jnp.min(jnp.where(se == mx, je, BIG), axis=0, keepdims=True),
                           jnp.min(jnp.where(so == mx, jo, BIG), axis=0, keepdims=True))
        hit_e = je == cand
        hit_o = jo == cand
        sel_e = jnp.where(hit_e, 1.0, sel_e)
        sel_o = jnp.where(hit_o, 1.0, sel_o)
        se = jnp.where(hit_e, -2.0, se)
        so = jnp.where(hit_o, -2.0, so)
    return jnp.where(valid_e, sel_e, 0.0), jnp.where(valid_o, sel_o, 0.0)


def _nsa_prompt_kernel(rb_ref, q_ref, ck_ref, cvT_ref, sk_ref, svT_ref, wk_ref, wvT_ref, gtT_ref,
                       bt_ref, bct_ref, ae_ref, ao_ref, o_ref, sc_ref, sele_ref, selo_ref, *, n_top):
    k = pl.program_id(0)
    i = pl.program_id(1)
    n_grp = ck_ref.shape[0]
    q = jnp.concatenate(
        [(q_ref[:, g * HEAD_DIM:(g + 1) * HEAD_DIM].astype(F32) * SCALE).astype(BF16)
         for g in range(NSA_GROUP)], axis=0)
    rb_far = [rb_ref[NUM_BUCKETS - 1, NSA_GROUP * k + g] for g in range(NSA_GROUP)]
    row = lax.broadcasted_iota(jnp.int32, (Q_BLOCK, Q_BLOCK), 0)
    col = lax.broadcasted_iota(jnp.int32, (Q_BLOCK, Q_BLOCK), 1)
    q_pos = Q_BLOCK * i + col
    gsl = lambda g: slice(g * Q_BLOCK, (g + 1) * Q_BLOCK)
    diag = i // QB_PER_CHUNK

    def cmp_scores(c, dst, near_w, live):
        cs = pl.multiple_of(c * CMP_CHUNK, CMP_CHUNK)
        s = _dot_nt(ck_ref[pl.ds(cs, CMP_CHUNK), :], q)
        if near_w is not None:
            valid = (CMP_STRIDE * (cs + row) + (CMP_BLOCK - 1) <= q_pos) & live
        parts = []
        for g in range(NSA_GROUP):
            if near_w is None:
                parts.append(s[:, gsl(g)] + rb_far[g])
            else:
                parts.append(jnp.where(valid, s[:, gsl(g)] + bct_ref[g, 0, near_w], NEG))
        s = jnp.concatenate(parts, axis=1)
        sc_ref[pl.ds(pl.multiple_of(dst, CMP_CHUNK), CMP_CHUNK), :] = s
        return jnp.max(s, axis=0, keepdims=True)

    m_c = lax.fori_loop(0, jnp.maximum(diag - 1, 0),
                        lambda c, m: jnp.maximum(m, cmp_scores(c, c * CMP_CHUNK, None, None)),
                        jnp.full((1, GQ), NEG, F32))
    has_prev = diag >= 1
    m_c = jnp.maximum(m_c, cmp_scores(jnp.maximum(diag - 1, 0),
                                      jnp.where(has_prev, (diag - 1) * CMP_CHUNK, n_grp), 0, has_prev))
    m_c = jnp.maximum(m_c, cmp_scores(diag, diag * CMP_CHUNK, 1, True))
    m_c = jnp.where(m_c > 0.5 * NEG, m_c, 0.0)

    def cmp_sum(c, l):
        cs = pl.multiple_of(c * CMP_CHUNK, CMP_CHUNK)
        return l + jnp.sum(jnp.exp(sc_ref[pl.ds(cs, CMP_CHUNK), :] - m_c), axis=0, keepdims=True)

    l_c = lax.fori_loop(0, diag + 1, cmp_sum, jnp.zeros((1, GQ), F32))
    r_c = 1.0 / jnp.maximum(l_c, 1e-30)

    n_half = sele_ref.shape[0]

    def cmp_out(c, carry):
        oc, ie, io = carry
        cs = pl.multiple_of(c * CMP_CHUNK, CMP_CHUNK)
        p = jnp.exp(sc_ref[pl.ds(cs, CMP_CHUNK), :] - m_c) * r_c
        oc = oc + _dot(cvT_ref[:, pl.ds(cs, CMP_CHUNK)], p.astype(BF16))
        pg = p[:, gsl(0)] + p[:, gsl(1)] + p[:, gsl(2)] + p[:, gsl(3)]
        ie = ie + _split_dot_left(ae_ref[:, pl.ds(cs, CMP_CHUNK)], pg)
        io = io + _split_dot_left(ao_ref[:, pl.ds(cs, CMP_CHUNK)], pg)
        return oc, ie, io

    o_c, imp_e, imp_o = lax.fori_loop(
        0, diag + 1, cmp_out,
        (jnp.zeros((HEAD_DIM, GQ), F32), jnp.zeros((n_half, Q_BLOCK), F32), jnp.zeros((n_half, Q_BLOCK), F32)))

    cur = (Q_BLOCK * i + lax.broadcasted_iota(jnp.int32, (n_half, Q_BLOCK), 1)) // SEL_BLOCK
    sel_e, sel_o = _topk_mask(imp_e, imp_o, cur, n_top)
    sele_ref[...] = sel_e
    selo_ref[...] = sel_o

    def attend(state, keys_ref, valsT_ref, c2, mask, bias):
        m, l, acc = state
        cs = pl.multiple_of(c2 * Q_BLOCK, Q_BLOCK)
        s = _dot_nt(keys_ref[pl.ds(cs, Q_BLOCK), :], q)
        s = jnp.concatenate([jnp.where(mask, s[:, gsl(g)] + bias(g), NEG) for g in range(NSA_GROUP)], axis=1)
        m_new = jnp.maximum(m, jnp.max(s, axis=0, keepdims=True))
        alpha = jnp.exp(m - m_new)
        p = jnp.exp(s - m_new)
        l = alpha * l + jnp.sum(p, axis=0, keepdims=True)
        acc = alpha * acc + _dot(valsT_ref[:, pl.ds(cs, Q_BLOCK)], p.astype(BF16))
        return m_new, l, acc

    def sel_mask(c2):
        me = sele_ref[pl.ds(c2, 1), :]
        mo = selo_ref[pl.ds(c2, 1), :]
        return jnp.where(row < SEL_BLOCK, me, mo) > 0.0

    init = (jnp.full((1, GQ), NEG, F32), jnp.zeros((1, GQ), F32), jnp.zeros((HEAD_DIM, GQ), F32))
    causal = row <= col

    st = init
    for d in range(NEAR_TILES):
        c2 = jnp.maximum(i - d, 0)
        mask = sel_mask(c2) & (i >= d)
        if d == 0:
            mask = mask & causal
        st = attend(st, sk_ref, svT_ref, c2, mask, lambda g, d=d: bt_ref[g, d])

    def sel_far(t, st):
        c2 = i - NEAR_TILES - t
        return attend(st, sk_ref, svT_ref, c2, sel_mask(c2), lambda g: rb_far[g])

    m_s, l_s, o_s = lax.fori_loop(0, jnp.maximum(i - NEAR_TILES + 1, 0), sel_far, st)

    st = init
    n_win = WINDOW // Q_BLOCK
    for d in range(n_win + 1):
        c2 = jnp.maximum(i - d, 0)
        mask = jnp.broadcast_to(i >= d, (Q_BLOCK, Q_BLOCK))
        if d == 0:
            mask = mask & causal
        if d == n_win:
            mask = mask & (row > col)
        st = attend(st, wk_ref, wvT_ref, c2, mask, lambda g, d=d: bt_ref[g, d])
    m_w, l_w, o_w = st

    r_s = 1.0 / jnp.maximum(l_s, 1e-30)
    r_w = 1.0 / jnp.maximum(l_w, 1e-30)
    for g in range(NSA_GROUP):
        gate = lambda br: jax.nn.sigmoid(gtT_ref[pl.ds(br * NSA_HEADS + NSA_GROUP * k + g, 1), :])
        o = (gate(0) * o_c[:, gsl(g)]
             + gate(1) * r_s[:, gsl(g)] * o_s[:, gsl(g)]
             + gate(2) * r_w[:, gsl(g)] * o_w[:, gsl(g)])
        o_ref[:, g * HEAD_DIM:(g + 1) * HEAD_DIM] = o.T


def _nsa_prompt(rel_bias, nq, ck, cvT, sk, svT, wk, wvT, gtT, bt, bct):
    t = nq.shape[0]
    n_grp = ck.shape[1]
    n_half = t // (2 * SEL_BLOCK)
    n_top = min(SEL_TOPK, t // SEL_BLOCK)
    a_e, a_o = _importance_matrices(n_half, n_grp)
    per_k3 = lambda shape: pl.BlockSpec((None,) + shape, lambda k, i: (k, 0, 0))
    const2 = lambda shape: pl.BlockSpec(shape, lambda k, i: (0, 0))
    return pl.pallas_call(
        functools.partial(_nsa_prompt_kernel, n_top=n_top),
        grid=(NSA_KV_HEADS, t // Q_BLOCK),
        in_specs=[pl.BlockSpec(memory_space=pltpu.SMEM),
                  pl.BlockSpec((Q_BLOCK, NSA_GROUP * HEAD_DIM), lambda k, i: (i, k)),
                  per_k3((n_grp, HEAD_DIM)), per_k3((HEAD_DIM, n_grp)),
                  per_k3((t, HEAD_DIM)), per_k3((HEAD_DIM, t)),
                  per_k3((t, HEAD_DIM)), per_k3((HEAD_DIM, t)),
                  pl.BlockSpec((GATE_W, Q_BLOCK), lambda k, i: (0, i)),
                  pl.BlockSpec((NSA_GROUP, NEAR_TILES, Q_BLOCK, Q_BLOCK), lambda k, i: (k, 0, 0, 0)),
                  pl.BlockSpec((NSA_GROUP, 1, 2, CMP_CHUNK, Q_BLOCK),
                               lambda k, i: (k, i % QB_PER_CHUNK, 0, 0, 0)),
                  const2((n_half, n_grp)), const2((n_half, n_grp))],
        out_specs=pl.BlockSpec((Q_BLOCK, NSA_GROUP * HEAD_DIM), lambda k, i: (i, k)),
        out_shape=jax.ShapeDtypeStruct((t, NSA_W), F32),
        scratch_shapes=[pltpu.VMEM((n_grp + CMP_CHUNK, GQ), F32),
                        pltpu.VMEM((n_half, Q_BLOCK), F32),
                        pltpu.VMEM((n_half, Q_BLOCK), F32)],
        compiler_params=_cparams(("parallel", "arbitrary")),
        name="nsa_prompt",
    )(rel_bias, nq, ck, cvT, sk, svT, wk, wvT, gtT, bt, bct, a_e, a_o)


def _per_kv_head(x):
    t = x.shape[0]
    xk = x.reshape(t, NSA_KV_HEADS, HEAD_DIM).transpose(1, 0, 2)
    return xk, xk.transpose(0, 2, 1)


def _nsa_prompt_group(rel_bias, f32, bf, cmp_w_k, cmp_w_v):
    ck = _compress_prompt(f32["ck"], *cmp_w_k)
    cv = _compress_prompt(f32["cv"], *cmp_w_v)
    ck_h, _ = _per_kv_head(ck.astype(BF16))
    _, cv_t = _per_kv_head(cv.astype(BF16))
    sk_h, _ = _per_kv_head(bf["sk2"])
    _, sv_t = _per_kv_head(bf["sv2"])
    wk_h, _ = _per_kv_head(bf["wk"])
    _, wv_t = _per_kv_head(bf["wv"])
    gt_t = f32["gt"][:, :GATE_W].T
    bt, bct = _bias_prompt(rel_bias)
    return _nsa_prompt(rel_bias, bf["nq"], ck_h, cv_t, sk_h, sv_t, wk_h, wv_t, gt_t, bt, bct)


PAGES_PER_STEP = 16
ROWS_S = NSA_HEADS * 4
KT = NSA_KV_HEADS * 4


def _paged(cache):
    n_pool, rows, heads, d = cache.shape
    return cache.transpose(0, 2, 3, 1).reshape(n_pool, heads * d, rows)


def _page_specs(n, width, page_of):
    return [pl.BlockSpec((None, width, PAGE_SIZE),
                         functools.partial(lambda b, p, pt, c: (page_of(b, p, c, pt), 0, 0), c=c))
            for c in range(n)]


def _bias_sample_kernel(rbr_ref, bsc_ref, bsw_ref, bss_ref, *, past, w_buf):
    t_of = lambda shape: lax.broadcasted_iota(jnp.int32, shape, 0) % 4
    lane = lambda shape: lax.broadcasted_iota(jnp.int32, shape, 1)
    rb_at = lambda b: rbr_ref[:, b:b + 1]
    shp = bsc_ref.shape
    bsc_ref[...] = _bias_from_dist(past + t_of(shp) - CMP_STRIDE * lane(shp) - (CMP_BLOCK - 1), rb_at)
    shp = bsw_ref.shape
    bsw_ref[...] = _bias_from_dist(w_buf + t_of(shp) - lane(shp), rb_at)
    shp = (ROWS_S, LANES)

    def chunk(c, carry):
        cs = pl.multiple_of(c * LANES, LANES)
        bss_ref[:, pl.ds(cs, LANES)] = _bias_from_dist(past + t_of(shp) - cs - lane(shp), rb_at)
        return carry

    lax.fori_loop(0, bss_ref.shape[1] // LANES, chunk, 0)


def _bias_sample(rbr, past, w_buf, n_cmp_pad, w_pad):
    return pl.pallas_call(
        functools.partial(_bias_sample_kernel, past=past, w_buf=w_buf),
        out_shape=[jax.ShapeDtypeStruct((ROWS_S, n_cmp_pad), F32),
                   jax.ShapeDtypeStruct((ROWS_S, w_pad), F32),
                   jax.ShapeDtypeStruct((ROWS_S, past + PAGE_SIZE), F32)],
        compiler_params=pltpu.CompilerParams(vmem_limit_bytes=VMEM_LIMIT),
        name="bias_sample",
    )(rbr)


def _masked_softmax_lanes(s, valid):
    s = jnp.where(valid, s, NEG)
    m = jnp.max(s, axis=-1, keepdims=True)
    m = jnp.where(m > 0.5 * NEG, m, 0.0)
    e = jnp.where(valid, jnp.exp(s - m), 0.0)
    return e / jnp.maximum(jnp.sum(e, axis=-1, keepdims=True), 1e-30)


def _own_kv_lanes(o, kv_of_row):
    return jnp.where(kv_of_row == 0, o[:, :HEAD_DIM], o[:, HEAD_DIM:])


def _topk_mask_lanes(imp, cur, n_blocks, n_top):
    j = lax.broadcasted_iota(jnp.int32, imp.shape, 1).astype(F32)
    curf = cur.astype(F32)
    valid = (j <= curf) & (j < float(n_blocks))
    forced = (j == 0.0) | (j == curf) | (j == curf - 1.0)
    score = jnp.where(valid, jnp.where(forced, BIG, imp), -1.0)
    sel = jnp.zeros(imp.shape, F32)
    for _ in range(n_top):
        mx = jnp.max(score, axis=1, keepdims=True)
        cand = jnp.min(jnp.where(score == mx, j, BIG), axis=1, keepdims=True)
        hit = j == cand
        sel = jnp.where(hit, 1.0, sel)
        score = jnp.where(hit, -2.0, score)
    return jnp.where(valid, sel, 0.0)


def _nsa_sample_cmp_kernel(pt_ref, *refs, past, n_top, n_sel):
    del pt_ref
    n = PAGES_PER_STEP
    kp = refs[:n]
    vp = refs[n:2 * n]
    (q_ref, nk_ref, nv_ref, wsk_ref, wsv_ref, nwk_ref, nwv_ref,
     wbk_ref, wpk_ref, posk_ref, w2k_ref, wbv_ref, wpv_ref, posv_ref, w2v_ref,
     at_ref, bsc_ref, bsw_ref,
     oc_ref, ow_ref, sel_ref,
     abk_ref, abv_ref, new8_ref, xt_ref, nwk128_ref, nwv128_ref) = refs[2 * n:]
    p = pl.program_id(1)
    n_steps = pl.num_programs(1)
    groups_per_page = PAGE_SIZE // CMP_STRIDE
    rows_per_step = n * groups_per_page
    n_cmp = abk_ref.shape[0] - 8
    w_buf = wsk_ref.shape[1]

    base = pl.multiple_of(p * rows_per_step, rows_per_step)
    for pages, wb_ref, ab_ref in ((kp, wbk_ref, abk_ref), (vp, wbv_ref, abv_ref)):
        for c in range(n):
            xt_ref[c * PAGE_SIZE:(c + 1) * PAGE_SIZE, :] = pages[c][...].T
        ab = jnp.zeros((rows_per_step, 2 * KV_W), F32)
        for l in range(CMP_STRIDE):
            x_l = xt_ref[pl.ds(l, rows_per_step, stride=CMP_STRIDE), :].astype(BF16)
            ab = ab + _dot(x_l, wb_ref[l * KV_W:(l + 1) * KV_W, :])
        ab_ref[pl.ds(base, rows_per_step), :] = ab

    @pl.when(p == n_steps - 1)
    def _():
        r8 = lax.broadcasted_iota(jnp.int32, (8, KV_W), 0)
        kv_of_row = (lax.broadcasted_iota(jnp.int32, (ROWS_S, 1), 0) % KT) // 4
        t_row = lax.broadcasted_iota(jnp.int32, (ROWS_S, 1), 0) % 4
        q = q_ref[...]

        comp = []
        for new_ref, wb_ref, ab_ref, wp_ref, pos_ref, w2_ref in (
                (nk_ref, wbk_ref, abk_ref, wpk_ref, posk_ref, w2k_ref),
                (nv_ref, wbv_ref, abv_ref, wpv_ref, posv_ref, w2v_ref)):
            new8_ref[...] = jnp.zeros((8, KV_W), F32)
            new8_ref[0:4, :] = new_ref[...]
            x = new8_ref[...]
            tot = jnp.zeros((8, 2 * KV_W), F32)
            for l in range(4):
                xl = jnp.where(r8 == l, x, 0.0).astype(BF16)
                tot = tot + _dot(xl, wb_ref[l * KV_W:(l + 1) * KV_W, :])
            ab_ref[n_cmp:n_cmp + 8, :] = jnp.where(
                lax.broadcasted_iota(jnp.int32, (8, 2 * KV_W), 0) == 0,
                jnp.sum(tot, axis=0, keepdims=True), 0.0)
            comp.append(_compress_finish(ab_ref, n_cmp, pos_ref[...], wp_ref[...], w2_ref[...]))
        ck, cv = comp

        lane_c = lax.broadcasted_iota(jnp.int32, (ROWS_S, n_cmp), 1)
        valid_c = past + t_row - CMP_STRIDE * lane_c - (CMP_BLOCK - 1) >= 0
        s_c = _dot_nt(q, ck.astype(BF16)) * SCALE + bsc_ref[...]
        p_c = _masked_softmax_lanes(s_c, valid_c)
        oc_ref[...] = _own_kv_lanes(_dot(p_c.astype(BF16), cv.astype(BF16)), kv_of_row)

        pg = p_c[0:KT] + p_c[KT:2 * KT] + p_c[2 * KT:3 * KT] + p_c[3 * KT:4 * KT]
        imp = _split_dot(pg, at_ref[...])
        cur = (past + lax.broadcasted_iota(jnp.int32, (KT, 1), 0) % 4) // SEL_BLOCK
        sel_ref[...] = _topk_mask_lanes(imp, cur, n_sel, n_top)

        for nw_ref, pad_ref in ((nwk_ref, nwk128_ref), (nwv_ref, nwv128_ref)):
            pad_ref[...] = jnp.zeros(pad_ref.shape, F32)
            pad_ref[0:4, :] = nw_ref[...]
        lane_w = lax.broadcasted_iota(jnp.int32, (ROWS_S, w_buf + LANES), 1)
        dist_w = w_buf + t_row - lane_w
        valid_w = (dist_w >= 0) & (dist_w < WINDOW) & (past - w_buf + lane_w >= 0)
        s_w = jnp.concatenate([_dot(q, wsk_ref[...].astype(BF16)),
                               _dot_nt(q, nwk128_ref[...].astype(BF16))], axis=1) * SCALE + bsw_ref[...]
        p_w = _masked_softmax_lanes(s_w, valid_w).astype(BF16)
        o_w = _dot_nt(p_w[:, :w_buf], wsv_ref[...].astype(BF16)) + _dot(p_w[:, w_buf:], nwv128_ref[...].astype(BF16))
        ow_ref[...] = _own_kv_lanes(o_w, kv_of_row)


def _nsa_sample_cmp(page_table, cache_k, cache_v, qn, new_k, new_v, win_k, win_v, new_wk, new_wv,
                    cmp_w_k, cmp_w_v, a_t, bsc, bsw, *, past):
    nb, n_pages = page_table.shape
    n = PAGES_PER_STEP
    n_steps = n_pages // n
    t_new = new_k.shape[1]
    assert past % CMP_STRIDE == 0 and t_new <= CMP_STRIDE
    n_cmp = past // CMP_STRIDE
    n_sel = -(-(past + t_new) // SEL_BLOCK)
    n_top = min(SEL_TOPK, n_sel)
    w_buf = win_k.shape[2]
    page_of = lambda b, p, c, pt: pt[b, p * n + c]
    per_b = lambda shape: pl.BlockSpec((None,) + shape, lambda b, p, pt: (b,) + (0,) * len(shape))
    const = lambda a: pl.BlockSpec(a.shape, lambda b, p, pt: (0,) * a.ndim)
    weights = list(cmp_w_k) + list(cmp_w_v)
    out_shape = [jax.ShapeDtypeStruct((nb, ROWS_S, HEAD_DIM), F32),
                 jax.ShapeDtypeStruct((nb, ROWS_S, HEAD_DIM), F32),
                 jax.ShapeDtypeStruct((nb, KT, a_t.shape[1]), F32)]
    grid_spec = pltpu.PrefetchScalarGridSpec(
        num_scalar_prefetch=1,
        grid=(nb, n_steps),
        in_specs=(_page_specs(n, KV_W, page_of) + _page_specs(n, KV_W, page_of)
                  + [per_b((ROWS_S, KV_W)), per_b((t_new, KV_W)), per_b((t_new, KV_W)),
                     per_b((KV_W, w_buf)), per_b((KV_W, w_buf)), per_b((t_new, KV_W)), per_b((t_new, KV_W))]
                  + [const(w) for w in weights] + [const(a_t), const(bsc), const(bsw)]),
        out_specs=[per_b(s.shape[1:]) for s in out_shape],
        scratch_shapes=[pltpu.VMEM((n_cmp + 8, 2 * KV_W), F32),
                        pltpu.VMEM((n_cmp + 8, 2 * KV_W), F32),
                        pltpu.VMEM((8, KV_W), F32),
                        pltpu.VMEM((n * PAGE_SIZE, KV_W), F32),
                        pltpu.VMEM((LANES, KV_W), F32),
                        pltpu.VMEM((LANES, KV_W), F32)])
    return pl.pallas_call(
        functools.partial(_nsa_sample_cmp_kernel, past=past, n_top=n_top, n_sel=n_sel),
        grid_spec=grid_spec,
        out_shape=out_shape,
        compiler_params=_cparams(("parallel", "arbitrary")),
        name="nsa_sample_cmp",
    )(page_table, *([cache_k] * n), *([cache_v] * n), qn, new_k, new_v, win_k, win_v, new_wk, new_wv,
      *weights, a_t, bsc, bsw)


def _nsa_sample_sel_kernel(pt_ref, *refs, past):
    del pt_ref
    n = PAGES_PER_STEP
    kp = refs[:n]
    vp = refs[n:2 * n]
    (q_ref, mask_ref, nmask_ref, nk_ref, nv_ref, bss_ref, oc_ref, ow_ref, gt_ref,
     o_ref, m_ref, l_ref, acc_ref, newk_ref, newv_ref) = refs[2 * n:]
    p = pl.program_id(1)
    n_steps = pl.num_programs(1)
    q = q_ref[...]
    lane = lax.broadcasted_iota(jnp.int32, (ROWS_S, PAGE_SIZE), 1)
    first_half = lane < SEL_BLOCK

    @pl.when(p == 0)
    def _():
        m_ref[...] = jnp.full(m_ref.shape, NEG, F32)
        l_ref[...] = jnp.zeros(l_ref.shape, F32)
        acc_ref[...] = jnp.zeros(acc_ref.shape, F32)

    def scores(qk, mk, bias, extra_valid=None):
        s = qk * SCALE + bias
        on = jnp.where(first_half, mk[:, 0:1], mk[:, 1:2]) > 0.0
        if extra_valid is not None:
            on = on & extra_valid
        return jnp.where(on, s, NEG)

    def update(s_list, v_list, paged):
        m_old = m_ref[...]
        m_new = m_old
        for s in s_list:
            m_new = jnp.maximum(m_new, jnp.max(s, axis=1, keepdims=True))
        alpha = jnp.exp(m_old - m_new)
        l = alpha * l_ref[...]
        acc = alpha * acc_ref[...]
        for s, v in zip(s_list, v_list):
            e = jnp.exp(s - m_new)
            l = l + jnp.sum(e, axis=1, keepdims=True)
            pv = _dot_nt if paged else _dot
            acc = acc + pv(e.astype(BF16), v.astype(BF16))
        m_ref[...] = m_new
        l_ref[...] = l
        acc_ref[...] = acc

    s_list = []
    for c in range(n):
        ks = pl.multiple_of((p * n + c) * PAGE_SIZE, PAGE_SIZE)
        s_list.append(scores(_dot(q, kp[c][...].astype(BF16)), mask_ref[c], bss_ref[:, pl.ds(ks, PAGE_SIZE)]))
    update(s_list, [r[...] for r in vp], True)

    @pl.when(p == n_steps - 1)
    def _():
        for src, dst in ((nk_ref, newk_ref), (nv_ref, newv_ref)):
            dst[...] = jnp.zeros(dst.shape, F32)
            dst[0:4, :] = src[...]
        t_row = lax.broadcasted_iota(jnp.int32, (ROWS_S, PAGE_SIZE), 0) % 4
        s_new = scores(_dot_nt(q, newk_ref[...].astype(BF16)), nmask_ref[...],
                       bss_ref[:, past:past + PAGE_SIZE], lane <= t_row)
        update([s_new], [newv_ref[...]], False)
        kv_of_row = (lax.broadcasted_iota(jnp.int32, (ROWS_S, 1), 0) % KT) // 4
        o_s = _own_kv_lanes(acc_ref[...] / jnp.maximum(l_ref[...], 1e-30), kv_of_row)
        gate = jax.nn.sigmoid(gt_ref[...])
        o_ref[...] = gate[:, 0:1] * oc_ref[...] + gate[:, 1:2] * o_s + gate[:, 2:3] * ow_ref[...]


def _nsa_sample_sel(page_table, cache_k, cache_v, qn, mask_pages, new_k, new_v, bss, o_c, o_w, gates, *, past):
    nb, n_pages = page_table.shape
    n = PAGES_PER_STEP
    n_steps = n_pages // n
    t_new = new_k.shape[1]
    ck2, cv2 = cache_k, cache_v
    page_of = lambda b, p, c, pt: pt[b, p * n + c]
    per_b = lambda shape: pl.BlockSpec((None,) + shape, lambda b, p, pt: (b,) + (0,) * len(shape))
    grid_spec = pltpu.PrefetchScalarGridSpec(
        num_scalar_prefetch=1,
        grid=(nb, n_steps),
        in_specs=(_page_specs(n, KV_W, page_of) + _page_specs(n, KV_W, page_of)
                  + [per_b((ROWS_S, KV_W)),
                     pl.BlockSpec((None, n, ROWS_S, 2), lambda b, p, pt: (b, p, 0, 0)),
                     pl.BlockSpec((None, None, ROWS_S, 2), lambda b, p, pt: (b, n_pages, 0, 0)),
                     per_b((t_new, KV_W)), per_b((t_new, KV_W)),
                     pl.BlockSpec(bss.shape, lambda b, p, pt: (0, 0)),
                     per_b((ROWS_S, HEAD_DIM)), per_b((ROWS_S, HEAD_DIM)), per_b((ROWS_S, 3))]),
        out_specs=per_b((ROWS_S, HEAD_DIM)),
        scratch_shapes=[pltpu.VMEM((ROWS_S, 1), F32), pltpu.VMEM((ROWS_S, 1), F32),
                        pltpu.VMEM((ROWS_S, KV_W), F32),
                        pltpu.VMEM((PAGE_SIZE, KV_W), F32), pltpu.VMEM((PAGE_SIZE, KV_W), F32)])
    return pl.pallas_call(
        functools.partial(_nsa_sample_sel_kernel, past=past),
        grid_spec=grid_spec,
        out_shape=jax.ShapeDtypeStruct((nb, ROWS_S, HEAD_DIM), F32),
        compiler_params=_cparams(("parallel", "arbitrary")),
        name="nsa_sample_sel",
    )(page_table, *([ck2] * n), *([cv2] * n), qn, mask_pages, mask_pages, new_k, new_v, bss, o_c, o_w, gates)


def _sb_sample_kernel(pt_ref, *refs):
    del pt_ref
    n = PAGES_PER_STEP
    kp = refs[:n]
    vp = refs[n:2 * n]
    q_ref, nk_ref, nv_ref, o_ref, ls_ref, acc_ref, newk_ref, newv_ref, top_ref = refs[2 * n:]
    p = pl.program_id(1)
    n_steps = pl.num_programs(1)
    q = q_ref[...]
    tri = _sb_tri()

    def block(keys, vals, valid):
        paged = valid is None
        qk = _dot if paged else _dot_nt
        pv = _dot_nt if paged else _dot
        z = qk(q, keys.astype(BF16)) * SCALE
        ln = jax.nn.log_sigmoid(-z)
        ln_m = ln if valid is None else jnp.where(valid, ln, 0.0)
        st = _split_dot(ln_m, tri)
        a = jnp.exp(z + ln + st[:, :PAGE_SIZE] + ls_ref[...])
        if valid is not None:
            a = jnp.where(valid, a, 0.0)
        acc_ref[...] += pv(a.astype(BF16), vals.astype(BF16))
        ls_ref[...] += st[:, PAGE_SIZE:]
        top_ref[0] = jnp.max(ls_ref[...])

    @pl.when(p == 0)
    def _():
        ls_ref[...] = jnp.zeros(ls_ref.shape, F32)
        acc_ref[...] = jnp.zeros(acc_ref.shape, F32)
        for src, dst in ((nk_ref, newk_ref), (nv_ref, newv_ref)):
            dst[...] = jnp.zeros(dst.shape, F32)
            dst[0:4, :] = src[...]
        lane = lax.broadcasted_iota(jnp.int32, (ROWS_S, PAGE_SIZE), 1)
        t_row = lax.broadcasted_iota(jnp.int32, (ROWS_S, PAGE_SIZE), 0) % 4
        block(newk_ref[...], newv_ref[...], lane < t_row)

    for c in range(n):
        @pl.when(top_ref[0] > EXP_ZERO)
        def _():
            block(kp[c][...], vp[c][...], None)

    @pl.when(p == n_steps - 1)
    def _():
        acc = acc_ref[...]
        head_of_row = lax.broadcasted_iota(jnp.int32, (ROWS_S, 1), 0) // 4
        o = jnp.zeros((ROWS_S, HEAD_DIM), F32)
        for h in range(SB_HEADS):
            o = jnp.where(head_of_row == h, acc[:, h * HEAD_DIM:(h + 1) * HEAD_DIM], o)
        o_ref[...] = o


def _sb_sample(page_table, cache_k, cache_v, qs, new_k, new_v):
    nb, n_pages = page_table.shape
    n = PAGES_PER_STEP
    n_steps = n_pages // n
    t_new = new_k.shape[1]
    ck2, cv2 = cache_k, cache_v
    page_of = lambda b, p, c, pt: pt[b, n_pages - 1 - (p * n + c)]
    per_b = lambda shape: pl.BlockSpec((None,) + shape, lambda b, p, pt: (b,) + (0,) * len(shape))
    grid_spec = pltpu.PrefetchScalarGridSpec(
        num_scalar_prefetch=1,
        grid=(nb, n_steps),
        in_specs=(_page_specs(n, SB_W, page_of) + _page_specs(n, SB_W, page_of)
                  + [per_b((ROWS_S, SB_W)), per_b((t_new, SB_W)), per_b((t_new, SB_W))]),
        out_specs=per_b((ROWS_S, HEAD_DIM)),
        scratch_shapes=[pltpu.VMEM((ROWS_S, PAGE_SIZE), F32), pltpu.VMEM((ROWS_S, SB_W), F32),
                        pltpu.VMEM((PAGE_SIZE, SB_W), F32), pltpu.VMEM((PAGE_SIZE, SB_W), F32),
                        pltpu.SMEM((1,), F32)])
    return pl.pallas_call(
        _sb_sample_kernel,
        grid_spec=grid_spec,
        out_shape=jax.ShapeDtypeStruct((nb, ROWS_S, HEAD_DIM), F32),
        compiler_params=_cparams(("parallel", "arbitrary")),
        name="sb_sample",
    )(page_table, *([ck2] * n), *([cv2] * n), qs, new_k, new_v)


def _sample_group(xs, f32, page_table, caches, states, rel_bias, cmp_w_k, cmp_w_v, past):
    cache_sb_k, cache_sb_v, cache_cmp_k, cache_cmp_v, cache_sel_k, cache_sel_v = caches
    win_k, win_v = states
    nb, n_pages = page_table.shape
    t_new = xs.shape[0] // nb
    rs = lambda a, w: a.reshape(nb, t_new, w)

    sq = f32["sq"].reshape(nb, t_new, SB_HEADS, HEAD_DIM).transpose(0, 2, 1, 3)
    qs = jnp.einsum("bhtd,hk->bhtkd", sq, jnp.eye(SB_HEADS, dtype=F32))
    qs = qs.reshape(nb, ROWS_S, SB_W).astype(BF16)
    nq = f32["nq"].reshape(nb, t_new, NSA_KV_HEADS, NSA_GROUP, HEAD_DIM).transpose(0, 3, 2, 1, 4)
    qn = jnp.einsum("bgktd,kj->bgktjd", nq, jnp.eye(NSA_KV_HEADS, dtype=F32))
    qn = qn.reshape(nb, ROWS_S, KV_W).astype(BF16)

    o_sb = _sb_sample(page_table, _paged(cache_sb_k), _paged(cache_sb_v), qs,
                      rs(f32["sk"], SB_W), rs(f32["sv"], SB_W))
    o_sb = o_sb.reshape(nb, SB_HEADS, t_new, HEAD_DIM).transpose(0, 2, 1, 3).reshape(nb * t_new, SB_W)

    n_cmp = past // CMP_STRIDE
    n_sel = -(-(past + t_new) // SEL_BLOCK)
    n_sel_pad = -(-2 * (n_pages + 1) // LANES) * LANES
    w_buf = win_k.shape[1]
    w_pad = w_buf + LANES
    nn = np.arange(n_cmp)[:, None]
    jj = np.arange(n_sel_pad)[None, :]
    a_t = jnp.asarray(((nn >= SEL_PER_CMP * jj - 1) & (nn <= SEL_PER_CMP * jj + SEL_PER_CMP - 1)
                       & (jj < n_sel)).astype(np.float32), BF16)
    r = np.arange(ROWS_S)
    head_of_row = ((r % KT) // 4) * NSA_GROUP + r // KT
    rbr = rel_bias.T[head_of_row]
    bsc, bsw, bss = _bias_sample(rbr, past, w_buf, n_cmp, w_pad)

    o_c, o_w, sel = _nsa_sample_cmp(
        page_table, _paged(cache_cmp_k), _paged(cache_cmp_v), qn, rs(f32["ck"], KV_W), rs(f32["cv"], KV_W),
        _paged(win_k), _paged(win_v), rs(f32["wk"], KV_W), rs(f32["wv"], KV_W),
        cmp_w_k, cmp_w_v, a_t, bsc, bsw, past=past)

    mask_pages = sel[:, :, :2 * (n_pages + 1)].reshape(nb, KT, n_pages + 1, 2).transpose(0, 2, 1, 3)
    mask_pages = jnp.tile(mask_pages, (1, 1, NSA_GROUP, 1))
    gates = f32["gt"][:, :GATE_W].reshape(nb, t_new, 3, NSA_KV_HEADS, NSA_GROUP)
    gates = gates.transpose(0, 4, 3, 1, 2).reshape(nb, ROWS_S, 3)
    o_nsa = _nsa_sample_sel(page_table, _paged(cache_sel_k), _paged(cache_sel_v), qn, mask_pages,
                            rs(f32["sk2"], KV_W), rs(f32["sv2"], KV_W), bss, o_c, o_w, gates, past=past)
    o_nsa = o_nsa.reshape(nb, NSA_GROUP, NSA_KV_HEADS, t_new, HEAD_DIM).transpose(0, 3, 2, 1, 4)
    return o_sb, o_nsa.reshape(nb * t_new, NSA_W)


def kernel(x_prompt, x_sample, cache_sb_k, cache_sb_v, cache_cmp_k, cache_cmp_v, cache_sel_k, cache_sel_v, state_win_k, state_win_v, page_table, ffn1_pre_g, ffn1_post_g, ffn1_w_gate, ffn1_w_up, ffn1_w_down, mix_pre_g, mix_post_g, w_in, cmp_pos_k, cmp_pos_v, w_cmp_k1, w_cmp_k2, w_cmp_v1, w_cmp_v2, sb_out_g, nsa_out_g, w_o, rel_bias, ffn2_pre_g, ffn2_post_g, ffn2_w_gate, ffn2_w_up, ffn2_w_down):
    depth = w_in.shape[0]
    assert depth == 1, "caches are updated for a single layer"
    nb_p, s_len, d_model = x_prompt.shape
    nb_s, t_new, _ = x_sample.shape
    assert nb_p == 1 and t_new == 4 and s_len % (CMP_CHUNK_SPAN) == 0
    n_pages = page_table.shape[1]
    past = n_pages * PAGE_SIZE
    assert n_pages % PAGES_PER_STEP == 0 and past % SEL_BLOCK == 0
    w_keep = min(WINDOW, s_len)
    tm_p = 512
    tm_s = min(512, nb_s * t_new)
    l = 0

    ffn1 = (ffn1_pre_g[l], ffn1_post_g[l], ffn1_w_gate[l].astype(BF16), ffn1_w_up[l].astype(BF16),
            ffn1_w_down[l].astype(BF16))
    ffn2 = (ffn2_pre_g[l], ffn2_post_g[l], ffn2_w_gate[l].astype(BF16), ffn2_w_up[l].astype(BF16),
            ffn2_w_down[l].astype(BF16))
    w_in_pad = _pad_w_in(w_in[l])
    w_o_b = w_o[l].astype(BF16)
    cmp_w_k = _compress_weights(w_cmp_k1[l], w_cmp_k2[l], cmp_pos_k[l])
    cmp_w_v = _compress_weights(w_cmp_v1[l], w_cmp_v2[l], cmp_pos_v[l])

    xp = _ffn(x_prompt[0], *ffn1, tm=tm_p)
    pf, pb = _in_proj(xp, mix_pre_g[l], w_in_pad, tm=tm_p)
    o_sb = _sb_prompt(pb["sq"], pb["sk"], pb["sv"])
    o_nsa = _nsa_prompt_group(rel_bias, pf, pb, cmp_w_k, cmp_w_v)
    xp = _out_proj(xp, o_sb, o_nsa, sb_out_g[l], nsa_out_g[l], w_o_b, mix_post_g[l], tm=tm_p)
    xp = _ffn(xp, *ffn2, tm=tm_p)

    xs = _ffn(x_sample.reshape(nb_s * t_new, d_model), *ffn1, tm=tm_s)
    sf, _ = _in_proj(xs, mix_pre_g[l], w_in_pad, tm=tm_s)
    caches = (cache_sb_k[l], cache_sb_v[l], cache_cmp_k[l], cache_cmp_v[l], cache_sel_k[l], cache_sel_v[l])
    win = (state_win_k[l], state_win_v[l])
    o_sb_s, o_nsa_s = _sample_group(xs, sf, page_table, caches, win, rel_bias, cmp_w_k, cmp_w_v, past)
    xs = _out_proj(xs, o_sb_s, o_nsa_s, sb_out_g[l], nsa_out_g[l], w_o_b, mix_post_g[l], tm=tm_s)
    xs = _ffn(xs, *ffn2, tm=tm_s)

    sbh = lambda a, n: a.reshape(1, -1, n, SB_HEADS, HEAD_DIM)
    kvh = lambda a, n: a.reshape(1, -1, n, NSA_KV_HEADS, HEAD_DIM)
    prompt_kv = [kvh(pf[name], s_len) for name in ("ck", "cv", "sk2", "sv2")]
    sample_kv = [kvh(sf[name], t_new) for name in ("ck", "cv", "sk2", "sv2")]
    new_wk = jnp.concatenate([state_win_k[l].reshape(nb_s, -1, KV_W), sf["wk"].reshape(nb_s, t_new, KV_W)],
                             axis=1)[:, t_new:]
    new_wv = jnp.concatenate([state_win_v[l].reshape(nb_s, -1, KV_W), sf["wv"].reshape(nb_s, t_new, KV_W)],
                             axis=1)[:, t_new:]
    return (xp[None], xs.reshape(nb_s, t_new, d_model),
            sbh(pf["sk"], s_len), sbh(pf["sv"], s_len), *prompt_kv,
            kvh(pf["wk"][s_len - w_keep:], w_keep), kvh(pf["wv"][s_len - w_keep:], w_keep),
            sbh(sf["sk"], t_new), sbh(sf["sv"], t_new), *sample_kv,
            kvh(new_wk, new_wk.shape[1]), kvh(new_wv, new_wv.shape[1]))
```

```python
import functools
import math

import numpy as np
import jax
import jax.numpy as jnp
from jax import lax
from jax.experimental import pallas as pl
from jax.experimental.pallas import tpu as pltpu

F32 = jnp.float32
BF16 = jnp.bfloat16

HEAD_DIM = 64
SB_HEADS = 8
NSA_HEADS = 8
NSA_KV_HEADS = 2
NSA_GROUP = 4
CMP_BLOCK = 32
CMP_STRIDE = 16
SEL_BLOCK = 64
SEL_TOPK = 16
SEL_PER_CMP = SEL_BLOCK // CMP_STRIDE
WINDOW = 512
Q_BLOCK = 128
NUM_BUCKETS = 32
MAX_EXACT = 16
MAX_DISTANCE = 1024
RMS_EPS = 1e-6
PAGE_SIZE = 128
SCALE = HEAD_DIM ** -0.5
LOG2E = math.log2(math.e)

LANES = 128
VMEM_LIMIT = 56 * 1024 * 1024
NEG = -1e30
EXP_ZERO = -104.0


def _bucket_thresholds():
    n = np.arange(0, 4 * MAX_DISTANCE)
    nf = np.maximum(n, 1).astype(np.float64)
    large = MAX_EXACT + (np.log(nf / MAX_EXACT) / math.log(MAX_DISTANCE / MAX_EXACT)
                         * (NUM_BUCKETS - MAX_EXACT)).astype(np.int64)
    bucket = np.where(n < MAX_EXACT, n, np.minimum(large, NUM_BUCKETS - 1))
    return [int(np.argmax(bucket >= b)) for b in range(1, NUM_BUCKETS)]


BUCKET_THR = _bucket_thresholds()
FAR_DIST = BUCKET_THR[-1]


def _cparams(sem):
    return pltpu.CompilerParams(dimension_semantics=sem, vmem_limit_bytes=VMEM_LIMIT)


def _dot(a, b):
    return jnp.dot(a, b, preferred_element_type=F32)


def _dot_nt(a, b):
    return lax.dot_general(a, b, (((1,), (1,)), ((), ())), preferred_element_type=F32)


def _dot_tn(a, b):
    return lax.dot_general(a, b, (((0,), (0,)), ((), ())), preferred_element_type=F32)


def _split_dot(x, w):
    hi = x.astype(BF16)
    lo = (x - hi.astype(F32)).astype(BF16)
    return _dot(hi, w) + _dot(lo, w)


def _split_dot_left(w, x):
    hi = x.astype(BF16)
    lo = (x - hi.astype(F32)).astype(BF16)
    return _dot(w, hi) + _dot(w, lo)


def _rms(x, g):
    return x * lax.rsqrt(jnp.mean(x * x, axis=-1, keepdims=True) + RMS_EPS) * g


def _ffn_kernel(x_ref, pre_ref, post_ref, wg_ref, wu_ref, wd_ref, o_ref, acc_ref, *, fc):
    x = x_ref[...]
    hb = _rms(x, pre_ref[...]).astype(BF16)
    n_f = wg_ref.shape[1]
    for c in range(n_f // fc):
        g = _dot(hb, wg_ref[:, c * fc:(c + 1) * fc])
        u = _dot(hb, wu_ref[:, c * fc:(c + 1) * fc])
        a = (jax.nn.silu(g) * u).astype(BF16)
        part = _dot(a, wd_ref[c * fc:(c + 1) * fc, :])
        if c == 0:
            acc_ref[...] = part
        else:
            acc_ref[...] += part
    o_ref[...] = x + 0.5 * _rms(acc_ref[...], post_ref[...])


def _ffn(x, pre_g, post_g, wg, wu, wd, *, tm):
    t, d = x.shape
    f = wg.shape[1]
    fc = 256
    const = lambda i: (0, 0)
    return pl.pallas_call(
        functools.partial(_ffn_kernel, fc=fc),
        grid=(t // tm,),
        in_specs=[
            pl.BlockSpec((tm, d), lambda i: (i, 0)),
            pl.BlockSpec((1, d), const),
            pl.BlockSpec((1, d), const),
            pl.BlockSpec((d, f), const, pipeline_mode=pl.Buffered(1)),
            pl.BlockSpec((d, f), const, pipeline_mode=pl.Buffered(1)),
            pl.BlockSpec((f, d), const, pipeline_mode=pl.Buffered(1)),
        ],
        out_specs=pl.BlockSpec((tm, d), lambda i: (i, 0)),
        out_shape=jax.ShapeDtypeStruct((t, d), F32),
        scratch_shapes=[pltpu.VMEM((tm, d), F32)],
        compiler_params=_cparams(("parallel",)),
        name="ffn_half",
    )(x, pre_g.reshape(1, d), post_g.reshape(1, d), wg, wu, wd)


SB_W = SB_HEADS * HEAD_DIM
NSA_W = NSA_HEADS * HEAD_DIM
KV_W = NSA_KV_HEADS * HEAD_DIM
GATE_W = 3 * NSA_HEADS
IN_PIECES = (("sq", SB_W), ("sk", SB_W), ("sv", SB_W), ("nq", NSA_W),
             ("ck", KV_W), ("cv", KV_W), ("sk2", KV_W), ("sv2", KV_W),
             ("wk", KV_W), ("wv", KV_W), ("gt", LANES))
IN_BF16 = ("sq", "sk", "sv", "nq", "sk2", "sv2", "wk", "wv")


def _in_proj_kernel(x_ref, g_ref, w_ref, *out_refs):
    hb = _rms(x_ref[...], g_ref[...]).astype(BF16)
    f32_refs = out_refs[:len(IN_PIECES)]
    bf_refs = dict(zip(IN_BF16, out_refs[len(IN_PIECES):]))
    off = 0
    for (name, width), o_ref in zip(IN_PIECES, f32_refs):
        p = _dot(hb, w_ref[:, off:off + width])
        o_ref[...] = p
        if name in bf_refs:
            bf_refs[name][...] = p.astype(BF16)
        off += width


def _in_proj(x, pre_g, w_in_pad, *, tm):
    t, d = x.shape
    wp = w_in_pad.shape[1]
    row = lambda i: (i, 0)
    const = lambda i: (0, 0)
    widths = dict(IN_PIECES)
    out_shape = ([jax.ShapeDtypeStruct((t, w), F32) for _, w in IN_PIECES]
                 + [jax.ShapeDtypeStruct((t, widths[n]), BF16) for n in IN_BF16])
    out_specs = ([pl.BlockSpec((tm, w), row) for _, w in IN_PIECES]
                 + [pl.BlockSpec((tm, widths[n]), row) for n in IN_BF16])
    outs = pl.pallas_call(
        _in_proj_kernel,
        grid=(t // tm,),
        in_specs=[pl.BlockSpec((tm, d), row),
                  pl.BlockSpec((1, d), const),
                  pl.BlockSpec((d, wp), const, pipeline_mode=pl.Buffered(1))],
        out_specs=out_specs,
        out_shape=out_shape,
        compiler_params=_cparams(("parallel",)),
        name="in_proj",
    )(x, pre_g.reshape(1, d), w_in_pad)
    f32 = {n: o for (n, _), o in zip(IN_PIECES, outs[:len(IN_PIECES)])}
    bf = dict(zip(IN_BF16, outs[len(IN_PIECES):]))
    return f32, bf


def _out_proj_kernel(x_ref, osb_ref, onsa_ref, sbg_ref, nsag_ref, wo_ref, post_ref, o_ref):
    a = _rms(osb_ref[...], sbg_ref[...]).astype(BF16)
    b = _rms(onsa_ref[...], nsag_ref[...]).astype(BF16)
    y = _dot(a, wo_ref[:SB_W, :]) + _dot(b, wo_ref[SB_W:, :])
    o_ref[...] = x_ref[...] + _rms(y, post_ref[...])


def _out_proj(x, o_sb, o_nsa, sb_g, nsa_g, w_o, post_g, *, tm):
    t, d = x.shape
    row = lambda i: (i, 0)
    const = lambda i: (0, 0)
    return pl.pallas_call(
        _out_proj_kernel,
        grid=(t // tm,),
        in_specs=[pl.BlockSpec((tm, d), row),
                  pl.BlockSpec((tm, SB_W), row),
                  pl.BlockSpec((tm, NSA_W), row),
                  pl.BlockSpec((1, SB_W), const),
                  pl.BlockSpec((1, NSA_W), const),
                  pl.BlockSpec((SB_W + NSA_W, d), const, pipeline_mode=pl.Buffered(1)),
                  pl.BlockSpec((1, d), const)],
        out_specs=pl.BlockSpec((tm, d), row),
        out_shape=jax.ShapeDtypeStruct((t, d), F32),
        compiler_params=_cparams(("parallel",)),
        name="out_proj",
    )(x, o_sb, o_nsa, sb_g.reshape(1, SB_W), nsa_g.reshape(1, NSA_W), w_o, post_g.reshape(1, d))


def _sb_tri():
    r = lax.broadcasted_iota(jnp.int32, (Q_BLOCK, 2 * Q_BLOCK), 0)
    c = lax.broadcasted_iota(jnp.int32, (Q_BLOCK, 2 * Q_BLOCK), 1)
    return jnp.where((c >= Q_BLOCK) | (r > c), 1.0, 0.0).astype(BF16)


SB_TILE = 2 * Q_BLOCK
SB_HEADS_PER_STEP = LANES // HEAD_DIM


def _sb_prompt_kernel(q_ref, k_ref, v_ref, o_ref, ls_ref, acc_ref):
    i = pl.program_id(1)
    r = lax.broadcasted_iota(jnp.int32, (SB_TILE, SB_TILE + Q_BLOCK), 0)
    c = lax.broadcasted_iota(jnp.int32, (SB_TILE, SB_TILE + Q_BLOCK), 1)
    tri = jnp.where((c >= SB_TILE) | (r > c), 1.0, 0.0).astype(BF16)
    q_pos = Q_BLOCK * i + lax.broadcasted_iota(jnp.int32, (Q_BLOCK, SB_TILE), 0)
    lane = lax.broadcasted_iota(jnp.int32, (Q_BLOCK, SB_TILE), 1)
    ls_ref[...] = jnp.zeros_like(ls_ref)
    acc_ref[...] = jnp.zeros_like(acc_ref)

    def tile(tt):
        hi = (i - 2 * tt + 1) * Q_BLOCK
        start = pl.multiple_of(jnp.maximum(hi - SB_TILE, 0), Q_BLOCK)
        k_pos = start + lane
        valid = (k_pos < q_pos) & (k_pos < hi)
        top = jnp.float32(NEG)
        for hh in range(SB_HEADS_PER_STEP):
            lanes = slice(hh * HEAD_DIM, (hh + 1) * HEAD_DIM)
            z = _dot_nt(q_ref[:, lanes], k_ref[pl.ds(start, SB_TILE), lanes]) * SCALE
            ln = jax.nn.log_sigmoid(-z)
            st = _split_dot(jnp.where(valid, ln, 0.0), tri)
            ls = ls_ref[hh]
            a = jnp.exp(z + ln + st[:, :SB_TILE] + jnp.concatenate([ls, ls], axis=1))
            a = jnp.where(valid, a, 0.0)
            acc_ref[hh] += _dot(a.astype(BF16), v_ref[pl.ds(start, SB_TILE), lanes])
            ls = ls + st[:, SB_TILE:]
            ls_ref[hh] = ls
            top = jnp.maximum(top, jnp.max(ls))
        return top

    def cond(carry):
        tt, top = carry
        return (i - 2 * tt + 1 > 0) & (top > EXP_ZERO)

    def body(carry):
        tt, _ = carry
        return tt + 1, tile(tt)

    lax.while_loop(cond, body, (1, tile(0)))
    for hh in range(SB_HEADS_PER_STEP):
        o_ref[:, hh * HEAD_DIM:(hh + 1) * HEAD_DIM] = acc_ref[hh]


def _sb_prompt(q, k, v):
    t = q.shape[0]
    assert t >= SB_TILE
    return pl.pallas_call(
        _sb_prompt_kernel,
        grid=(SB_W // LANES, t // Q_BLOCK),
        in_specs=[pl.BlockSpec((Q_BLOCK, LANES), lambda h, i: (i, h)),
                  pl.BlockSpec((t, LANES), lambda h, i: (0, h)),
                  pl.BlockSpec((t, LANES), lambda h, i: (0, h))],
        out_specs=pl.BlockSpec((Q_BLOCK, LANES), lambda h, i: (i, h)),
        out_shape=jax.ShapeDtypeStruct((t, SB_W), F32),
        scratch_shapes=[pltpu.VMEM((SB_HEADS_PER_STEP, Q_BLOCK, Q_BLOCK), F32),
                        pltpu.VMEM((SB_HEADS_PER_STEP, Q_BLOCK, HEAD_DIM), F32)],
        compiler_params=_cparams(("parallel", "parallel")),
        name="sb_prompt",
    )(q, k, v)


def _pad_w_in(w_in):
    wp = sum(w for _, w in IN_PIECES)
    return jnp.pad(w_in, ((0, 0), (0, wp - w_in.shape[1]))).astype(BF16)


def _bias_from_dist(dist, rb_at):
    out = jnp.where(dist >= BUCKET_THR[0], rb_at(1), rb_at(0))
    for b in range(2, NUM_BUCKETS):
        out = jnp.where(dist >= BUCKET_THR[b - 1], rb_at(b), out)
    return out


CMP_CHUNK = 128
CMP_CHUNK_SPAN = CMP_CHUNK * CMP_STRIDE
QB_PER_CHUNK = CMP_CHUNK_SPAN // Q_BLOCK
NEAR_TILES = -(-(FAR_DIST + Q_BLOCK - 1) // Q_BLOCK)
FAR_TILE = NEAR_TILES
WIN_TAIL_TILE = NEAR_TILES + 1
DEAD_TILE = NEAR_TILES + 2
BIAS_TILES = NEAR_TILES + 3
WIN_BLOCKS = WINDOW // Q_BLOCK


def _bias_prompt_kernel(rb_ref, bt_ref, bct_ref):
    h = pl.program_id(0)
    row = lax.broadcasted_iota(jnp.int32, (Q_BLOCK, Q_BLOCK), 0)
    col = lax.broadcasted_iota(jnp.int32, (Q_BLOCK, Q_BLOCK), 1)
    rb_at = lambda b: rb_ref[b, h]

    def tile(d, carry):
        bias = _bias_from_dist(Q_BLOCK * d + col - row, rb_at) * LOG2E
        bt_ref[d] = jnp.where((d > 0) | (row <= col), bias, NEG)
        return carry

    lax.fori_loop(0, FAR_TILE + 1, tile, 0)
    tail = _bias_from_dist(Q_BLOCK * WIN_BLOCKS + col - row, rb_at) * LOG2E
    bt_ref[WIN_TAIL_TILE] = jnp.where(row > col, tail, NEG)
    bt_ref[DEAD_TILE] = jnp.full((Q_BLOCK, Q_BLOCK), NEG, F32)

    def ctile(rw, carry):
        r = rw // 2
        w = rw % 2
        dist = (Q_BLOCK * r + col - CMP_STRIDE * row - (CMP_BLOCK - 1)
                + CMP_CHUNK_SPAN * (1 - w))
        bct_ref[r, w] = _bias_from_dist(dist, rb_at)
        return carry

    lax.fori_loop(0, 2 * QB_PER_CHUNK, ctile, 0)


def _bias_prompt(rel_bias):
    nh = rel_bias.shape[1]
    return pl.pallas_call(
        _bias_prompt_kernel,
        grid=(nh,),
        in_specs=[pl.BlockSpec(memory_space=pltpu.SMEM)],
        out_specs=[pl.BlockSpec((None, BIAS_TILES, Q_BLOCK, Q_BLOCK), lambda h: (h, 0, 0, 0)),
                   pl.BlockSpec((None, QB_PER_CHUNK, 2, CMP_CHUNK, Q_BLOCK), lambda h: (h, 0, 0, 0, 0))],
        out_shape=[jax.ShapeDtypeStruct((nh, BIAS_TILES, Q_BLOCK, Q_BLOCK), F32),
                   jax.ShapeDtypeStruct((nh, QB_PER_CHUNK, 2, CMP_CHUNK, Q_BLOCK), F32)],
        compiler_params=_cparams(("parallel",)),
        name="bias_prompt",
    )(rel_bias)


GROUP_W = CMP_STRIDE * KV_W


def _compress_weights(w1, w2, pos):
    eye = jnp.eye(NSA_KV_HEADS, dtype=w1.dtype)
    halves = []
    for half in range(2):
        wh = w1[half * CMP_STRIDE:(half + 1) * CMP_STRIDE]
        halves.append(jnp.einsum("lde,pk->lpdke", wh, eye).reshape(GROUP_W, KV_W))
    w_big = jnp.concatenate(halves, axis=1).astype(BF16)
    w_pos = jnp.tile(w1.reshape(CMP_BLOCK * HEAD_DIM, HEAD_DIM), (1, NSA_KV_HEADS)).astype(BF16)
    pos_row = jnp.pad(pos.reshape(1, CMP_BLOCK * HEAD_DIM), ((0, 7), (0, 0))).astype(BF16)
    w2_bd = jnp.einsum("ef,pk->pekf", w2, eye).reshape(KV_W, KV_W).astype(BF16)
    return w_big, _pos_term(pos_row, w_pos), w2_bd


def _pos_term_kernel(pos_ref, w_ref, o_ref):
    o_ref[...] = _dot(pos_ref[...], w_ref[...])


def _pos_term(pos_row, w_pos):
    return pl.pallas_call(
        _pos_term_kernel,
        out_shape=jax.ShapeDtypeStruct((pos_row.shape[0], KV_W), F32),
        name="pos_term",
    )(pos_row, w_pos)


def _compress_finish(ab_ref, n_rows, c_rows, w2_bd):
    h = ab_ref[0:n_rows, 0:KV_W] + ab_ref[1:n_rows + 1, KV_W:2 * KV_W] + c_rows[0:1, :]
    return _dot(jax.nn.silu(h).astype(BF16), w2_bd)


def _compress_kernel(g_ref, wbig_ref, c_ref, w2_ref, o_ref, ab_ref):
    n = g_ref.shape[0]
    ab_ref[0:n, :] = _dot(g_ref[...].astype(BF16), wbig_ref[...])
    ab_ref[n:n + 8, :] = jnp.zeros((8, 2 * KV_W), F32)
    o_ref[...] = _compress_finish(ab_ref, n, c_ref[...], w2_ref[...])


def _compress_prompt(rows, w_big, c_rows, w2_bd):
    g = rows.reshape(rows.shape[0] // CMP_STRIDE, GROUP_W)
    n = g.shape[0]
    return pl.pallas_call(
        _compress_kernel,
        out_shape=jax.ShapeDtypeStruct((n, KV_W), F32),
        scratch_shapes=[pltpu.VMEM((n + 8, 2 * KV_W), F32)],
        compiler_params=pltpu.CompilerParams(vmem_limit_bytes=VMEM_LIMIT),
        name="compress_prompt",
    )(g, w_big, c_rows, w2_bd)


BIG = 3e38
GQ = NSA_GROUP * Q_BLOCK


def _importance_matrices(n_half, n_cmp):
    r = np.arange(n_half)[:, None]
    n = np.arange(n_cmp)[None, :]
    mats = []
    for parity in range(2):
        j = 2 * r + parity
        mats.append(((n >= SEL_PER_CMP * j - 1) & (n <= SEL_PER_CMP * j + SEL_PER_CMP - 1)).astype(np.float32))
    return jnp.asarray(mats[0], BF16), jnp.asarray(mats[1], BF16)


def _topk_mask(imp_e, imp_o, cur, n_top):
    shape = imp_e.shape
    jr = lax.broadcasted_iota(jnp.int32, shape, 0)
    je = (2 * jr).astype(F32)
    jo = (2 * jr + 1).astype(F32)
    curf = cur.astype(F32)

    def encode(imp, j):
        valid = j <= curf
        forced = (j == 0.0) | (j == curf) | (j == curf - 1.0)
        return jnp.where(valid, jnp.where(forced, BIG, imp), -1.0), valid

    se, valid_e = encode(imp_e, je)
    so, valid_o = encode(imp_o, jo)
    sel_e = jnp.zeros(shape, F32)
    sel_o = jnp.zeros(shape, F32)
    for _ in range(n_top):
        mx = jnp.maximum(jnp.max(se, axis=0, keepdims=True), jnp.max(so, axis=0, keepdims=True))
        cand = jnp.minimum(jnp.min(jnp.where(se == mx, je, BIG), axis=0, keepdims=True),
                           jnp.min(jnp.where(so == mx, jo, BIG), axis=0, keepdims=True))
        hit_e = je == cand
        hit_o = jo == cand
        sel_e = jnp.where(hit_e, 1.0, sel_e)
        sel_o = jnp.where(hit_o, 1.0, sel_o)
        se = jnp.where(hit_e, -2.0, se)
        so = jnp.where(hit_o, -2.0, so)
    return jnp.where(valid_e, sel_e, 0.0), jnp.where(valid_o, sel_o, 0.0)


def _nsa_prompt_kernel(rb_ref, q_ref, ck_ref, cvT_ref, sk_ref, svT_ref, wk_ref, wvT_ref, gtT_ref,
                       bt_ref, bct_ref, ae_ref, ao_ref, o_ref, sc_ref, sele_ref, selo_ref, *, n_top):
    k = pl.program_id(0)
    i = pl.program_id(1)
    n_grp = ck_ref.shape[0]
    q = jnp.concatenate(
        [(q_ref[:, g * HEAD_DIM:(g + 1) * HEAD_DIM].astype(F32) * SCALE).astype(BF16)
         for g in range(NSA_GROUP)], axis=0)
    rb_far = [rb_ref[NUM_BUCKETS - 1, NSA_GROUP * k + g] for g in range(NSA_GROUP)]
    row = lax.broadcasted_iota(jnp.int32, (Q_BLOCK, Q_BLOCK), 0)
    col = lax.broadcasted_iota(jnp.int32, (Q_BLOCK, Q_BLOCK), 1)
    q_pos = Q_BLOCK * i + col
    gsl = lambda g: slice(g * Q_BLOCK, (g + 1) * Q_BLOCK)
    diag = i // QB_PER_CHUNK

    def cmp_scores(c, dst, near_w, live):
        cs = pl.multiple_of(c * CMP_CHUNK, CMP_CHUNK)
        s = _dot_nt(ck_ref[pl.ds(cs, CMP_CHUNK), :], q)
        if near_w is not None:
            valid = (CMP_STRIDE * (cs + row) + (CMP_BLOCK - 1) <= q_pos) & live
        parts = []
        for g in range(NSA_GROUP):
            if near_w is None:
                parts.append(s[:, gsl(g)] + rb_far[g])
            else:
                parts.append(jnp.where(valid, s[:, gsl(g)] + bct_ref[g, 0, near_w], NEG))
        s = jnp.concatenate(parts, axis=1)
        sc_ref[pl.ds(pl.multiple_of(dst, CMP_CHUNK), CMP_CHUNK), :] = s
        return jnp.max(s, axis=0, keepdims=True)

    m_c = lax.fori_loop(0, jnp.maximum(diag - 1, 0),
                        lambda c, m: jnp.maximum(m, cmp_scores(c, c * CMP_CHUNK, None, None)),
                        jnp.full((1, GQ), NEG, F32))
    has_prev = diag >= 1
    m_c = jnp.maximum(m_c, cmp_scores(jnp.maximum(diag - 1, 0),
                                      jnp.where(has_prev, (diag - 1) * CMP_CHUNK, n_grp), 0, has_prev))
    m_c = jnp.maximum(m_c, cmp_scores(diag, diag * CMP_CHUNK, 1, True))
    m_c = jnp.where(m_c > 0.5 * NEG, m_c, 0.0)

    def cmp_sum(c, l):
        cs = pl.multiple_of(c * CMP_CHUNK, CMP_CHUNK)
        return l + jnp.sum(jnp.exp(sc_ref[pl.ds(cs, CMP_CHUNK), :] - m_c), axis=0, keepdims=True)

    l_c = lax.fori_loop(0, diag + 1, cmp_sum, jnp.zeros((1, GQ), F32))
    r_c = 1.0 / jnp.maximum(l_c, 1e-30)

    n_half = sele_ref.shape[0]

    def cmp_out(c, carry):
        oc, ie, io = carry
        cs = pl.multiple_of(c * CMP_CHUNK, CMP_CHUNK)
        p = jnp.exp(sc_ref[pl.ds(cs, CMP_CHUNK), :] - m_c) * r_c
        oc = oc + _dot(cvT_ref[:, pl.ds(cs, CMP_CHUNK)], p.astype(BF16))
        pg = p[:, gsl(0)] + p[:, gsl(1)] + p[:, gsl(2)] + p[:, gsl(3)]
        ie = ie + _split_dot_left(ae_ref[:, pl.ds(cs, CMP_CHUNK)], pg)
        io = io + _split_dot_left(ao_ref[:, pl.ds(cs, CMP_CHUNK)], pg)
        return oc, ie, io

    o_c, imp_e, imp_o = lax.fori_loop(
        0, diag + 1, cmp_out,
        (jnp.zeros((HEAD_DIM, GQ), F32), jnp.zeros((n_half, Q_BLOCK), F32), jnp.zeros((n_half, Q_BLOCK), F32)))

    cur = (Q_BLOCK * i + lax.broadcasted_iota(jnp.int32, (n_half, Q_BLOCK), 1)) // SEL_BLOCK
    sel_e, sel_o = _topk_mask(imp_e, imp_o, cur, n_top)
    sele_ref[...] = sel_e
    selo_ref[...] = sel_o

    q2 = jnp.concatenate(
        [(q_ref[:, g * HEAD_DIM:(g + 1) * HEAD_DIM].astype(F32) * (SCALE * LOG2E)).astype(BF16)
         for g in range(NSA_GROUP)], axis=0)
    def scores(keys_ref, d):
        cs = pl.multiple_of(jnp.maximum(i - d, 0) * Q_BLOCK, Q_BLOCK)
        return _dot_nt(keys_ref[pl.ds(cs, Q_BLOCK), :], q2)

    def values(valsT_ref, d):
        cs = pl.multiple_of(jnp.maximum(i - d, 0) * Q_BLOCK, Q_BLOCK)
        return valsT_ref[:, pl.ds(cs, Q_BLOCK)]

    def update(s, tile, extra, m, l, acc):
        parts = []
        for g in range(NSA_GROUP):
            add = bt_ref[g, tile] if extra is None else bt_ref[g, tile] + extra
            parts.append(s[:, gsl(g)] + add)
        s = jnp.concatenate(parts, axis=1)
        m_new = jnp.maximum(m, jnp.max(s, axis=0, keepdims=True))
        alpha = jnp.exp2(m - m_new)
        p = jnp.exp2(s - m_new)
        l = alpha * l + jnp.sum(p, axis=0, keepdims=True)
        return p.astype(BF16), m_new, l, alpha * acc

    def tile_of(d, last):
        return jnp.where(d <= i, jnp.minimum(d, last), DEAD_TILE)

    def sel_mask(d):
        c2 = jnp.maximum(i - d, 0)
        off = lambda r: jnp.broadcast_to(jnp.where(r > 0.0, 0.0, NEG), (SEL_BLOCK, Q_BLOCK))
        return jnp.concatenate([off(sele_ref[pl.ds(c2, 1), :]), off(selo_ref[pl.ds(c2, 1), :])], axis=0)

    m0 = jnp.full((1, GQ), NEG, F32)
    l0 = jnp.zeros((1, GQ), F32)
    acc0 = jnp.zeros((HEAD_DIM, GQ), F32)

    def sel_pair(t, carry):
        s_a, p_prev, m, l, acc = carry
        d_a = 2 * t
        s_b = scores(sk_ref, d_a + 1)
        acc = acc + _dot(values(svT_ref, jnp.maximum(d_a - 1, 0)), p_prev)
        p_a, m, l, acc = update(s_a, tile_of(d_a, FAR_TILE), sel_mask(d_a), m, l, acc)
        s_next = scores(sk_ref, d_a + 2)
        acc = acc + _dot(values(svT_ref, d_a), p_a)
        p_b, m, l, acc = update(s_b, tile_of(d_a + 1, FAR_TILE), sel_mask(d_a + 1), m, l, acc)
        return s_next, p_b, m, l, acc

    n_pairs = i // 2 + 1
    _, p_last, _, l_s, o_s = lax.fori_loop(
        0, n_pairs, sel_pair, (scores(sk_ref, 0), jnp.zeros((Q_BLOCK, GQ), BF16), m0, l0, acc0))
    o_s = o_s + _dot(values(svT_ref, 2 * n_pairs - 1), p_last)

    s_w = [scores(wk_ref, d) for d in range(WIN_BLOCKS + 1)]
    m, l_w, o_w = m0, l0, acc0
    for d in range(WIN_BLOCKS + 1):
        tile = tile_of(d, FAR_TILE) if d < WIN_BLOCKS else jnp.where(d <= i, WIN_TAIL_TILE, DEAD_TILE)
        p, m, l_w, o_w = update(s_w[d], tile, None, m, l_w, o_w)
        o_w = o_w + _dot(values(wvT_ref, d), p)

    r_s = 1.0 / jnp.maximum(l_s, 1e-30)
    r_w = 1.0 / jnp.maximum(l_w, 1e-30)
    for g in range(NSA_GROUP):
        gate = lambda br: jax.nn.sigmoid(gtT_ref[pl.ds(br * NSA_HEADS + NSA_GROUP * k + g, 1), :])
        o = (gate(0) * o_c[:, gsl(g)]
             + gate(1) * r_s[:, gsl(g)] * o_s[:, gsl(g)]
             + gate(2) * r_w[:, gsl(g)] * o_w[:, gsl(g)])
        o_ref[:, g * HEAD_DIM:(g + 1) * HEAD_DIM] = o.T


def _nsa_prompt(rel_bias, nq, ck, cvT, sk, svT, wk, wvT, gtT, bt, bct):
    t = nq.shape[0]
    n_grp = ck.shape[1]
    n_half = t // (2 * SEL_BLOCK)
    n_top = min(SEL_TOPK, t // SEL_BLOCK)
    a_e, a_o = _importance_matrices(n_half, n_grp)
    per_k3 = lambda shape: pl.BlockSpec((None,) + shape, lambda k, i: (k, 0, 0))
    const2 = lambda shape: pl.BlockSpec(shape, lambda k, i: (0, 0))
    return pl.pallas_call(
        functools.partial(_nsa_prompt_kernel, n_top=n_top),
        grid=(NSA_KV_HEADS, t // Q_BLOCK),
        in_specs=[pl.BlockSpec(memory_space=pltpu.SMEM),
                  pl.BlockSpec((Q_BLOCK, NSA_GROUP * HEAD_DIM), lambda k, i: (i, k)),
                  per_k3((n_grp, HEAD_DIM)), per_k3((HEAD_DIM, n_grp)),
                  per_k3((t, HEAD_DIM)), per_k3((HEAD_DIM, t)),
                  per_k3((t, HEAD_DIM)), per_k3((HEAD_DIM, t)),
                  pl.BlockSpec((GATE_W, Q_BLOCK), lambda k, i: (0, i)),
                  pl.BlockSpec((NSA_GROUP, BIAS_TILES, Q_BLOCK, Q_BLOCK), lambda k, i: (k, 0, 0, 0)),
                  pl.BlockSpec((NSA_GROUP, 1, 2, CMP_CHUNK, Q_BLOCK),
                               lambda k, i: (k, i % QB_PER_CHUNK, 0, 0, 0)),
                  const2((n_half, n_grp)), const2((n_half, n_grp))],
        out_specs=pl.BlockSpec((Q_BLOCK, NSA_GROUP * HEAD_DIM), lambda k, i: (i, k)),
        out_shape=jax.ShapeDtypeStruct((t, NSA_W), F32),
        scratch_shapes=[pltpu.VMEM((n_grp + CMP_CHUNK, GQ), F32),
                        pltpu.VMEM((n_half, Q_BLOCK), F32),
                        pltpu.VMEM((n_half, Q_BLOCK), F32)],
        compiler_params=_cparams(("parallel", "arbitrary")),
        name="nsa_prompt",
    )(rel_bias, nq, ck, cvT, sk, svT, wk, wvT, gtT, bt, bct, a_e, a_o)


def _per_kv_head(x):
    t = x.shape[0]
    xk = x.reshape(t, NSA_KV_HEADS, HEAD_DIM).transpose(1, 0, 2)
    return xk, xk.transpose(0, 2, 1)


def _nsa_prompt_group(rel_bias, f32, bf, cmp_w_k, cmp_w_v):
    ck = _compress_prompt(f32["ck"], *cmp_w_k)
    cv = _compress_prompt(f32["cv"], *cmp_w_v)
    ck_h, _ = _per_kv_head(ck.astype(BF16))
    _, cv_t = _per_kv_head(cv.astype(BF16))
    sk_h, _ = _per_kv_head(bf["sk2"])
    _, sv_t = _per_kv_head(bf["sv2"])
    wk_h, _ = _per_kv_head(bf["wk"])
    _, wv_t = _per_kv_head(bf["wv"])
    gt_t = f32["gt"][:, :GATE_W].T
    bt, bct = _bias_prompt(rel_bias)
    return _nsa_prompt(rel_bias, bf["nq"], ck_h, cv_t, sk_h, sv_t, wk_h, wv_t, gt_t, bt, bct)


PAGES_PER_STEP = 32
ROWS_S = NSA_HEADS * 4
KT = NSA_KV_HEADS * 4


def _paged(cache):
    n_pool, rows, heads, d = cache.shape
    return cache.transpose(0, 2, 3, 1).reshape(n_pool, heads * d, rows)


def _page_specs(n, width, page_of):
    return [pl.BlockSpec((None, width, PAGE_SIZE),
                         functools.partial(lambda b, p, pt, c: (page_of(b, p, c, pt), 0, 0), c=c))
            for c in range(n)]


def _bias_sample_kernel(rbr_ref, bsc_ref, bsw_ref, bss_ref, *, past, w_buf):
    t_of = lambda shape: lax.broadcasted_iota(jnp.int32, shape, 0) % 4
    lane = lambda shape: lax.broadcasted_iota(jnp.int32, shape, 1)
    rb_at = lambda b: rbr_ref[:, b:b + 1]
    shp = bsc_ref.shape
    bsc_ref[...] = _bias_from_dist(past + t_of(shp) - CMP_STRIDE * lane(shp) - (CMP_BLOCK - 1), rb_at)
    shp = bsw_ref.shape
    bsw_ref[...] = _bias_from_dist(w_buf + t_of(shp) - lane(shp), rb_at)
    shp = (ROWS_S, LANES)

    def chunk(c, carry):
        cs = pl.multiple_of(c * LANES, LANES)
        bss_ref[:, pl.ds(cs, LANES)] = _bias_from_dist(past + t_of(shp) - cs - lane(shp), rb_at)
        return carry

    lax.fori_loop(0, bss_ref.shape[1] // LANES, chunk, 0)


def _bias_sample(rbr, past, w_buf, n_cmp_pad, w_pad):
    return pl.pallas_call(
        functools.partial(_bias_sample_kernel, past=past, w_buf=w_buf),
        out_shape=[jax.ShapeDtypeStruct((ROWS_S, n_cmp_pad), F32),
                   jax.ShapeDtypeStruct((ROWS_S, w_pad), F32),
                   jax.ShapeDtypeStruct((ROWS_S, past + PAGE_SIZE), F32)],
        compiler_params=pltpu.CompilerParams(vmem_limit_bytes=VMEM_LIMIT),
        name="bias_sample",
    )(rbr)


def _masked_softmax_lanes(s, valid):
    s = jnp.where(valid, s, NEG)
    m = jnp.max(s, axis=-1, keepdims=True)
    m = jnp.where(m > 0.5 * NEG, m, 0.0)
    e = jnp.where(valid, jnp.exp(s - m), 0.0)
    return e / jnp.maximum(jnp.sum(e, axis=-1, keepdims=True), 1e-30)


def _own_kv_lanes(o, kv_of_row):
    return jnp.where(kv_of_row == 0, o[:, :HEAD_DIM], o[:, HEAD_DIM:])


def _topk_mask_rows(imp, cur, n_blocks, n_top):
    j = lax.broadcasted_iota(jnp.int32, imp.shape, 0).astype(F32)
    curf = cur.astype(F32)
    valid = (j <= curf) & (j < float(n_blocks))
    forced = (j == 0.0) | (j == curf) | (j == curf - 1.0)
    score = jnp.where(valid, jnp.where(forced, BIG, imp), -1.0)
    sel = jnp.zeros(imp.shape, F32)
    for _ in range(n_top):
        mx = jnp.max(score, axis=0, keepdims=True)
        cand = jnp.min(jnp.where(score == mx, j, BIG), axis=0, keepdims=True)
        hit = j == cand
        sel = jnp.where(hit, 1.0, sel)
        score = jnp.where(hit, -2.0, score)
    return jnp.where(valid, sel, 0.0)


def _nsa_sample_cmp_kernel(pt_ref, *refs, past, n_top, n_sel):
    del pt_ref
    n = PAGES_PER_STEP
    kp = refs[:n]
    vp = refs[n:2 * n]
    (q_ref, nk_ref, nv_ref, wsk_ref, wsv_ref, nwk_ref, nwv_ref,
     wbk_ref, ck_ref, w2k_ref, wbv_ref, cv_ref, w2v_ref,
     at_ref, bsc_ref, bsw_ref,
     oc_ref, ow_ref, sel_ref,
     abk_ref, abv_ref, new8_ref, xt_ref, nwk128_ref, nwv128_ref) = refs[2 * n:]
    p = pl.program_id(1)
    n_steps = pl.num_programs(1)
    groups_per_page = PAGE_SIZE // CMP_STRIDE
    rows_per_step = n * groups_per_page
    n_cmp = abk_ref.shape[0] - 8
    w_buf = wsk_ref.shape[1]

    base = pl.multiple_of(p * rows_per_step, rows_per_step)
    pr = lax.broadcasted_iota(jnp.int32, (PAGE_SIZE, PAGE_SIZE), 0)
    pc = lax.broadcasted_iota(jnp.int32, (PAGE_SIZE, PAGE_SIZE), 1)
    perm = (pc == CMP_STRIDE * (pr % groups_per_page) + pr // groups_per_page).astype(BF16)
    for pages, wb_ref, ab_ref in ((kp, wbk_ref, abk_ref), (vp, wbv_ref, abv_ref)):
        for c in range(0, n, 2):
            two = jnp.concatenate([pages[c][...], pages[c + 1][...]], axis=0).astype(BF16)
            xt = _dot_nt(perm, two)
            xt_ref[c] = xt[:, :KV_W]
            xt_ref[c + 1] = xt[:, KV_W:]
        ab = jnp.zeros((rows_per_step, 2 * KV_W), F32)
        for l in range(0, CMP_STRIDE, 2):
            x_l = xt_ref[:, l * groups_per_page:(l + 2) * groups_per_page, :]
            x_l = jnp.concatenate([x_l[:, :groups_per_page].reshape(rows_per_step, KV_W),
                                   x_l[:, groups_per_page:].reshape(rows_per_step, KV_W)], axis=1)
            ab = ab + _dot(x_l.astype(BF16), wb_ref[l * KV_W:(l + 2) * KV_W, :])
        ab_ref[pl.ds(base, rows_per_step), :] = ab

    @pl.when(p == n_steps - 1)
    def _():
        r8 = lax.broadcasted_iota(jnp.int32, (8, KV_W), 0)
        kv_of_row = (lax.broadcasted_iota(jnp.int32, (ROWS_S, 1), 0) % KT) // 4
        t_row = lax.broadcasted_iota(jnp.int32, (ROWS_S, 1), 0) % 4
        q = q_ref[...]

        comp = []
        for new_ref, wb_ref, ab_ref, c_ref, w2_ref in (
                (nk_ref, wbk_ref, abk_ref, ck_ref, w2k_ref),
                (nv_ref, wbv_ref, abv_ref, cv_ref, w2v_ref)):
            new8_ref[...] = jnp.zeros((8, KV_W), F32)
            new8_ref[0:4, :] = new_ref[...]
            x = new8_ref[...]
            tot = jnp.zeros((8, 2 * KV_W), F32)
            for l in range(4):
                xl = jnp.where(r8 == l, x, 0.0).astype(BF16)
                tot = tot + _dot(xl, wb_ref[l * KV_W:(l + 1) * KV_W, :])
            ab_ref[n_cmp:n_cmp + 8, :] = jnp.where(
                lax.broadcasted_iota(jnp.int32, (8, 2 * KV_W), 0) == 0,
                jnp.sum(tot, axis=0, keepdims=True), 0.0)
            comp.append(_compress_finish(ab_ref, n_cmp, c_ref[...], w2_ref[...]))
        ck, cv = comp

        lane_c = lax.broadcasted_iota(jnp.int32, (ROWS_S, n_cmp), 1)
        valid_c = past + t_row - CMP_STRIDE * lane_c - (CMP_BLOCK - 1) >= 0
        s_c = _dot_nt(q, ck.astype(BF16)) * SCALE + bsc_ref[...]
        p_c = _masked_softmax_lanes(s_c, valid_c)
        oc_ref[...] = _own_kv_lanes(_dot(p_c.astype(BF16), cv.astype(BF16)), kv_of_row)

        pg = p_c[0:KT] + p_c[KT:2 * KT] + p_c[2 * KT:3 * KT] + p_c[3 * KT:4 * KT]
        pg = jnp.concatenate([pg, jnp.zeros((LANES - KT, n_cmp), F32)], axis=0)
        pg_hi = pg.astype(BF16)
        pg_lo = (pg - pg_hi.astype(F32)).astype(BF16)
        imp = _dot_nt(at_ref[...], pg_hi) + _dot_nt(at_ref[...], pg_lo)
        cur = (past + lax.broadcasted_iota(jnp.int32, (1, LANES), 1) % 4) // SEL_BLOCK
        sel_ref[...] = _topk_mask_rows(imp, cur, n_sel, n_top)[:, :KT]

        for nw_ref, pad_ref in ((nwk_ref, nwk128_ref), (nwv_ref, nwv128_ref)):
            pad_ref[...] = jnp.zeros(pad_ref.shape, F32)
            pad_ref[0:4, :] = nw_ref[...]
        lane_w = lax.broadcasted_iota(jnp.int32, (ROWS_S, w_buf + LANES), 1)
        dist_w = w_buf + t_row - lane_w
        valid_w = (dist_w >= 0) & (dist_w < WINDOW) & (past - w_buf + lane_w >= 0)
        s_w = jnp.concatenate([_dot(q, wsk_ref[...].astype(BF16)),
                               _dot_nt(q, nwk128_ref[...].astype(BF16))], axis=1) * SCALE + bsw_ref[...]
        p_w = _masked_softmax_lanes(s_w, valid_w).astype(BF16)
        o_w = _dot_nt(p_w[:, :w_buf], wsv_ref[...].astype(BF16)) + _dot(p_w[:, w_buf:], nwv128_ref[...].astype(BF16))
        ow_ref[...] = _own_kv_lanes(o_w, kv_of_row)


def _nsa_sample_cmp(page_table, cache_k, cache_v, qn, new_k, new_v, win_k, win_v, new_wk, new_wv,
                    cmp_w_k, cmp_w_v, a_t, bsc, bsw, *, past):
    nb, n_pages = page_table.shape
    n = PAGES_PER_STEP
    n_steps = n_pages // n
    t_new = new_k.shape[1]
    assert past % CMP_STRIDE == 0 and t_new <= CMP_STRIDE
    n_cmp = past // CMP_STRIDE
    n_sel = -(-(past + t_new) // SEL_BLOCK)
    n_top = min(SEL_TOPK, n_sel)
    w_buf = win_k.shape[2]
    page_of = lambda b, p, c, pt: pt[b, p * n + c]
    per_b = lambda shape: pl.BlockSpec((None,) + shape, lambda b, p, pt: (b,) + (0,) * len(shape))
    const = lambda a: pl.BlockSpec(a.shape, lambda b, p, pt: (0,) * a.ndim)
    weights = list(cmp_w_k) + list(cmp_w_v)
    out_shape = [jax.ShapeDtypeStruct((nb, ROWS_S, HEAD_DIM), F32),
                 jax.ShapeDtypeStruct((nb, ROWS_S, HEAD_DIM), F32),
                 jax.ShapeDtypeStruct((nb, a_t.shape[0], KT), F32)]
    grid_spec = pltpu.PrefetchScalarGridSpec(
        num_scalar_prefetch=1,
        grid=(nb, n_steps),
        in_specs=(_page_specs(n, KV_W, page_of) + _page_specs(n, KV_W, page_of)
                  + [per_b((ROWS_S, KV_W)), per_b((t_new, KV_W)), per_b((t_new, KV_W)),
                     per_b((KV_W, w_buf)), per_b((KV_W, w_buf)), per_b((t_new, KV_W)), per_b((t_new, KV_W))]
                  + [const(w) for w in weights] + [const(a_t), const(bsc), const(bsw)]),
        out_specs=[per_b(s.shape[1:]) for s in out_shape],
        scratch_shapes=[pltpu.VMEM((n_cmp + 8, 2 * KV_W), F32),
                        pltpu.VMEM((n_cmp + 8, 2 * KV_W), F32),
                        pltpu.VMEM((8, KV_W), F32),
                        pltpu.VMEM((n, PAGE_SIZE, KV_W), F32),
                        pltpu.VMEM((LANES, KV_W), F32),
                        pltpu.VMEM((LANES, KV_W), F32)])
    return pl.pallas_call(
        functools.partial(_nsa_sample_cmp_kernel, past=past, n_top=n_top, n_sel=n_sel),
        grid_spec=grid_spec,
        out_shape=out_shape,
        compiler_params=_cparams(("parallel", "arbitrary")),
        name="nsa_sample_cmp",
    )(page_table, *([cache_k] * n), *([cache_v] * n), qn, new_k, new_v, win_k, win_v, new_wk, new_wv,
      *weights, a_t, bsc, bsw)


def _nsa_sample_sel_kernel(pt_ref, *refs, past):
    del pt_ref
    n = PAGES_PER_STEP
    kp = refs[:n]
    vp = refs[n:2 * n]
    (q_ref, mask_ref, nmask_ref, nk_ref, nv_ref, bss_ref, oc_ref, ow_ref, gt_ref,
     o_ref, m_ref, l_ref, acc_ref, newk_ref, newv_ref) = refs[2 * n:]
    p = pl.program_id(1)
    n_steps = pl.num_programs(1)
    q = q_ref[...]
    lane = lax.broadcasted_iota(jnp.int32, (ROWS_S, PAGE_SIZE), 1)
    first_half = lane < SEL_BLOCK

    @pl.when(p == 0)
    def _():
        m_ref[...] = jnp.full(m_ref.shape, NEG, F32)
        l_ref[...] = jnp.zeros(l_ref.shape, F32)
        acc_ref[...] = jnp.zeros(acc_ref.shape, F32)

    def scores(qk, mk, bias, extra_valid=None):
        s = qk * SCALE + bias
        on = jnp.where(first_half, mk[:, 0:1], mk[:, 1:2]) > 0.0
        if extra_valid is not None:
            on = on & extra_valid
        return jnp.where(on, s, NEG)

    def update(s_list, v_list, paged):
        m_old = m_ref[...]
        m_new = m_old
        for s in s_list:
            m_new = jnp.maximum(m_new, jnp.max(s, axis=1, keepdims=True))
        alpha = jnp.exp(m_old - m_new)
        l = alpha * l_ref[...]
        acc = alpha * acc_ref[...]
        for s, v in zip(s_list, v_list):
            e = jnp.exp(s - m_new)
            l = l + jnp.sum(e, axis=1, keepdims=True)
            pv = _dot_nt if paged else _dot
            acc = acc + pv(e.astype(BF16), v.astype(BF16))
        m_ref[...] = m_new
        l_ref[...] = l
        acc_ref[...] = acc

    s_list = []
    for c in range(n):
        ks = pl.multiple_of((p * n + c) * PAGE_SIZE, PAGE_SIZE)
        s_list.append(scores(_dot(q, kp[c][...].astype(BF16)), mask_ref[c], bss_ref[:, pl.ds(ks, PAGE_SIZE)]))
    update(s_list, [r[...] for r in vp], True)

    @pl.when(p == n_steps - 1)
    def _():
        for src, dst in ((nk_ref, newk_ref), (nv_ref, newv_ref)):
            dst[...] = jnp.zeros(dst.shape, F32)
            dst[0:4, :] = src[...]
        t_row = lax.broadcasted_iota(jnp.int32, (ROWS_S, PAGE_SIZE), 0) % 4
        s_new = scores(_dot_nt(q, newk_ref[...].astype(BF16)), nmask_ref[...],
                       bss_ref[:, past:past + PAGE_SIZE], lane <= t_row)
        update([s_new], [newv_ref[...]], False)
        kv_of_row = (lax.broadcasted_iota(jnp.int32, (ROWS_S, 1), 0) % KT) // 4
        o_s = _own_kv_lanes(acc_ref[...] / jnp.maximum(l_ref[...], 1e-30), kv_of_row)
        gate = jax.nn.sigmoid(gt_ref[...])
        o_ref[...] = gate[:, 0:1] * oc_ref[...] + gate[:, 1:2] * o_s + gate[:, 2:3] * ow_ref[...]


def _nsa_sample_sel(page_table, cache_k, cache_v, qn, mask_pages, new_k, new_v, bss, o_c, o_w, gates, *, past):
    nb, n_pages = page_table.shape
    n = PAGES_PER_STEP
    n_steps = n_pages // n
    t_new = new_k.shape[1]
    ck2, cv2 = cache_k, cache_v
    page_of = lambda b, p, c, pt: pt[b, p * n + c]
    per_b = lambda shape: pl.BlockSpec((None,) + shape, lambda b, p, pt: (b,) + (0,) * len(shape))
    grid_spec = pltpu.PrefetchScalarGridSpec(
        num_scalar_prefetch=1,
        grid=(nb, n_steps),
        in_specs=(_page_specs(n, KV_W, page_of) + _page_specs(n, KV_W, page_of)
                  + [per_b((ROWS_S, KV_W)),
                     pl.BlockSpec((None, n, ROWS_S, 2), lambda b, p, pt: (b, p, 0, 0)),
                     pl.BlockSpec((None, None, ROWS_S, 2), lambda b, p, pt: (b, n_pages, 0, 0)),
                     per_b((t_new, KV_W)), per_b((t_new, KV_W)),
                     pl.BlockSpec(bss.shape, lambda b, p, pt: (0, 0)),
                     per_b((ROWS_S, HEAD_DIM)), per_b((ROWS_S, HEAD_DIM)), per_b((ROWS_S, 3))]),
        out_specs=per_b((ROWS_S, HEAD_DIM)),
        scratch_shapes=[pltpu.VMEM((ROWS_S, 1), F32), pltpu.VMEM((ROWS_S, 1), F32),
                        pltpu.VMEM((ROWS_S, KV_W), F32),
                        pltpu.VMEM((PAGE_SIZE, KV_W), F32), pltpu.VMEM((PAGE_SIZE, KV_W), F32)])
    return pl.pallas_call(
        functools.partial(_nsa_sample_sel_kernel, past=past),
        grid_spec=grid_spec,
        out_shape=jax.ShapeDtypeStruct((nb, ROWS_S, HEAD_DIM), F32),
        compiler_params=_cparams(("parallel", "arbitrary")),
        name="nsa_sample_sel",
    )(page_table, *([ck2] * n), *([cv2] * n), qn, mask_pages, mask_pages, new_k, new_v, bss, o_c, o_w, gates)


SB_AHEAD = 2


def _sb_sample_kernel(pt_ref, q_ref, nk_ref, nv_ref, ck_hbm, cv_hbm, o_ref,
                      ls_ref, acc_ref, newk_ref, newv_ref, top_ref,
                      kbuf, vbuf, ksem, vsem, kx, vx, xsem):
    b = pl.program_id(0)
    nb = pl.num_programs(0)
    n_pages = pt_ref.shape[1]
    slot = b % 2
    q = q_ref[...]
    tri = _sb_tri()

    def ahead(seq, sl, j):
        page = pt_ref[seq, n_pages - 1 - j]
        return (pltpu.make_async_copy(ck_hbm.at[page], kbuf.at[sl, j], ksem.at[sl, j]),
                pltpu.make_async_copy(cv_hbm.at[page], vbuf.at[sl, j], vsem.at[sl, j]))

    def start_ahead(seq, sl):
        for j in range(SB_AHEAD):
            for cp in ahead(seq, sl, j):
                cp.start()

    @pl.when(b == 0)
    def _():
        start_ahead(0, 0)

    @pl.when(b + 1 < nb)
    def _():
        start_ahead(b + 1, 1 - slot)

    def block(keys, vals, valid):
        paged = valid is None
        qk = _dot if paged else _dot_nt
        pv = _dot_nt if paged else _dot
        z = qk(q, keys.astype(BF16)) * SCALE
        ln = jax.nn.log_sigmoid(-z)
        ln_m = ln if valid is None else jnp.where(valid, ln, 0.0)
        st = _split_dot(ln_m, tri)
        a = jnp.exp(z + ln + st[:, :PAGE_SIZE] + ls_ref[...])
        if valid is not None:
            a = jnp.where(valid, a, 0.0)
        acc_ref[...] += pv(a.astype(BF16), vals.astype(BF16))
        ls_ref[...] += st[:, PAGE_SIZE:]
        top_ref[0] = jnp.max(ls_ref[...])

    ls_ref[...] = jnp.zeros(ls_ref.shape, F32)
    acc_ref[...] = jnp.zeros(acc_ref.shape, F32)
    for src, dst in ((nk_ref, newk_ref), (nv_ref, newv_ref)):
        dst[...] = jnp.zeros(dst.shape, F32)
        dst[0:4, :] = src[...]
    lane = lax.broadcasted_iota(jnp.int32, (ROWS_S, PAGE_SIZE), 1)
    t_row = lax.broadcasted_iota(jnp.int32, (ROWS_S, PAGE_SIZE), 0) % 4
    block(newk_ref[...], newv_ref[...], lane < t_row)

    for j in range(SB_AHEAD):
        for cp in ahead(b, slot, j):
            cp.wait()

        @pl.when(top_ref[0] > EXP_ZERO)
        def _():
            block(kbuf[slot, j], vbuf[slot, j], None)

    def more(carry):
        j, top = carry
        return (j < n_pages) & (top > EXP_ZERO)

    def fetch(carry):
        j, _ = carry
        page = pt_ref[b, n_pages - 1 - j]
        ck = pltpu.make_async_copy(ck_hbm.at[page], kx, xsem.at[0])
        cv = pltpu.make_async_copy(cv_hbm.at[page], vx, xsem.at[1])
        ck.start()
        cv.start()
        ck.wait()
        cv.wait()
        block(kx[...], vx[...], None)
        return j + 1, top_ref[0]

    lax.while_loop(more, fetch, (SB_AHEAD, top_ref[0]))

    acc = acc_ref[...]
    head_of_row = lax.broadcasted_iota(jnp.int32, (ROWS_S, 1), 0) // 4
    o = jnp.zeros((ROWS_S, HEAD_DIM), F32)
    for h in range(SB_HEADS):
        o = jnp.where(head_of_row == h, acc[:, h * HEAD_DIM:(h + 1) * HEAD_DIM], o)
    o_ref[...] = o


def _sb_sample(page_table, cache_k, cache_v, qs, new_k, new_v):
    nb, n_pages = page_table.shape
    assert n_pages >= SB_AHEAD
    t_new = new_k.shape[1]
    per_b = lambda shape: pl.BlockSpec((None,) + shape, lambda b, pt: (b,) + (0,) * len(shape))
    page = (SB_W, PAGE_SIZE)
    grid_spec = pltpu.PrefetchScalarGridSpec(
        num_scalar_prefetch=1,
        grid=(nb,),
        in_specs=[per_b((ROWS_S, SB_W)), per_b((t_new, SB_W)), per_b((t_new, SB_W)),
                  pl.BlockSpec(memory_space=pl.ANY), pl.BlockSpec(memory_space=pl.ANY)],
        out_specs=per_b((ROWS_S, HEAD_DIM)),
        scratch_shapes=[pltpu.VMEM((ROWS_S, PAGE_SIZE), F32), pltpu.VMEM((ROWS_S, SB_W), F32),
                        pltpu.VMEM((PAGE_SIZE, SB_W), F32), pltpu.VMEM((PAGE_SIZE, SB_W), F32),
                        pltpu.SMEM((1,), F32),
                        pltpu.VMEM((2, SB_AHEAD) + page, F32), pltpu.VMEM((2, SB_AHEAD) + page, F32),
                        pltpu.SemaphoreType.DMA((2, SB_AHEAD)), pltpu.SemaphoreType.DMA((2, SB_AHEAD)),
                        pltpu.VMEM(page, F32), pltpu.VMEM(page, F32), pltpu.SemaphoreType.DMA((2,))])
    return pl.pallas_call(
        _sb_sample_kernel,
        grid_spec=grid_spec,
        out_shape=jax.ShapeDtypeStruct((nb, ROWS_S, HEAD_DIM), F32),
        compiler_params=_cparams(("arbitrary",)),
        name="sb_sample",
    )(page_table, qs, new_k, new_v, cache_k, cache_v)


def _sample_group(xs, f32, page_table, caches, states, rel_bias, cmp_w_k, cmp_w_v, past):
    cache_sb_k, cache_sb_v, cache_cmp_k, cache_cmp_v, cache_sel_k, cache_sel_v = caches
    win_k, win_v = states
    nb, n_pages = page_table.shape
    t_new = xs.shape[0] // nb
    rs = lambda a, w: a.reshape(nb, t_new, w)

    sq = f32["sq"].reshape(nb, t_new, SB_HEADS, HEAD_DIM).transpose(0, 2, 1, 3)
    qs = jnp.einsum("bhtd,hk->bhtkd", sq, jnp.eye(SB_HEADS, dtype=F32))
    qs = qs.reshape(nb, ROWS_S, SB_W).astype(BF16)
    nq = f32["nq"].reshape(nb, t_new, NSA_KV_HEADS, NSA_GROUP, HEAD_DIM).transpose(0, 3, 2, 1, 4)
    qn = jnp.einsum("bgktd,kj->bgktjd", nq, jnp.eye(NSA_KV_HEADS, dtype=F32))
    qn = qn.reshape(nb, ROWS_S, KV_W).astype(BF16)

    o_sb = _sb_sample(page_table, _paged(cache_sb_k), _paged(cache_sb_v), qs,
                      rs(f32["sk"], SB_W), rs(f32["sv"], SB_W))
    o_sb = o_sb.reshape(nb, SB_HEADS, t_new, HEAD_DIM).transpose(0, 2, 1, 3).reshape(nb * t_new, SB_W)

    n_cmp = past // CMP_STRIDE
    n_sel = -(-(past + t_new) // SEL_BLOCK)
    n_sel_pad = -(-2 * (n_pages + 1) // LANES) * LANES
    w_buf = win_k.shape[1]
    w_pad = w_buf + LANES
    nn = np.arange(n_cmp)[None, :]
    jj = np.arange(n_sel_pad)[:, None]
    a_t = jnp.asarray(((nn >= SEL_PER_CMP * jj - 1) & (nn <= SEL_PER_CMP * jj + SEL_PER_CMP - 1)
                       & (jj < n_sel)).astype(np.float32), BF16)
    r = np.arange(ROWS_S)
    head_of_row = ((r % KT) // 4) * NSA_GROUP + r // KT
    rbr = rel_bias.T[head_of_row]
    bsc, bsw, bss = _bias_sample(rbr, past, w_buf, n_cmp, w_pad)

    o_c, o_w, sel = _nsa_sample_cmp(
        page_table, _paged(cache_cmp_k), _paged(cache_cmp_v), qn, rs(f32["ck"], KV_W), rs(f32["cv"], KV_W),
        _paged(win_k), _paged(win_v), rs(f32["wk"], KV_W), rs(f32["wv"], KV_W),
        cmp_w_k, cmp_w_v, a_t, bsc, bsw, past=past)

    mask_pages = sel[:, :2 * (n_pages + 1)].reshape(nb, n_pages + 1, 2, KT).transpose(0, 1, 3, 2)
    mask_pages = jnp.tile(mask_pages, (1, 1, NSA_GROUP, 1))
    gates = f32["gt"][:, :GATE_W].reshape(nb, t_new, 3, NSA_KV_HEADS, NSA_GROUP)
    gates = gates.transpose(0, 4, 3, 1, 2).reshape(nb, ROWS_S, 3)
    o_nsa = _nsa_sample_sel(page_table, _paged(cache_sel_k), _paged(cache_sel_v), qn, mask_pages,
                            rs(f32["sk2"], KV_W), rs(f32["sv2"], KV_W), bss, o_c, o_w, gates, past=past)
    o_nsa = o_nsa.reshape(nb, NSA_GROUP, NSA_KV_HEADS, t_new, HEAD_DIM).transpose(0, 3, 2, 1, 4)
    return o_sb, o_nsa.reshape(nb * t_new, NSA_W)


def kernel(x_prompt, x_sample, cache_sb_k, cache_sb_v, cache_cmp_k, cache_cmp_v, cache_sel_k, cache_sel_v, state_win_k, state_win_v, page_table, ffn1_pre_g, ffn1_post_g, ffn1_w_gate, ffn1_w_up, ffn1_w_down, mix_pre_g, mix_post_g, w_in, cmp_pos_k, cmp_pos_v, w_cmp_k1, w_cmp_k2, w_cmp_v1, w_cmp_v2, sb_out_g, nsa_out_g, w_o, rel_bias, ffn2_pre_g, ffn2_post_g, ffn2_w_gate, ffn2_w_up, ffn2_w_down):
    depth = w_in.shape[0]
    assert depth == 1, "caches are updated for a single layer"
    nb_p, s_len, d_model = x_prompt.shape
    nb_s, t_new, _ = x_sample.shape
    assert nb_p == 1 and t_new == 4 and s_len % (CMP_CHUNK_SPAN) == 0
    n_pages = page_table.shape[1]
    past = n_pages * PAGE_SIZE
    assert n_pages % PAGES_PER_STEP == 0 and past % SEL_BLOCK == 0
    w_keep = min(WINDOW, s_len)
    tm_p = 512
    tm_s = min(512, nb_s * t_new)
    l = 0

    ffn1 = (ffn1_pre_g[l], ffn1_post_g[l], ffn1_w_gate[l].astype(BF16), ffn1_w_up[l].astype(BF16),
            ffn1_w_down[l].astype(BF16))
    ffn2 = (ffn2_pre_g[l], ffn2_post_g[l], ffn2_w_gate[l].astype(BF16), ffn2_w_up[l].astype(BF16),
            ffn2_w_down[l].astype(BF16))
    w_in_pad = _pad_w_in(w_in[l])
    w_o_b = w_o[l].astype(BF16)
    cmp_w_k = _compress_weights(w_cmp_k1[l], w_cmp_k2[l], cmp_pos_k[l])
    cmp_w_v = _compress_weights(w_cmp_v1[l], w_cmp_v2[l], cmp_pos_v[l])

    xp = _ffn(x_prompt[0], *ffn1, tm=tm_p)
    pf, pb = _in_proj(xp, mix_pre_g[l], w_in_pad, tm=tm_p)
    o_sb = _sb_prompt(pb["sq"], pb["sk"], pb["sv"])
    o_nsa = _nsa_prompt_group(rel_bias, pf, pb, cmp_w_k, cmp_w_v)
    xp = _out_proj(xp, o_sb, o_nsa, sb_out_g[l], nsa_out_g[l], w_o_b, mix_post_g[l], tm=tm_p)
    xp = _ffn(xp, *ffn2, tm=tm_p)

    xs = _ffn(x_sample.reshape(nb_s * t_new, d_model), *ffn1, tm=tm_s)
    sf, _ = _in_proj(xs, mix_pre_g[l], w_in_pad, tm=tm_s)
    caches = (cache_sb_k[l], cache_sb_v[l], cache_cmp_k[l], cache_cmp_v[l], cache_sel_k[l], cache_sel_v[l])
    win = (state_win_k[l], state_win_v[l])
    o_sb_s, o_nsa_s = _sample_group(xs, sf, page_table, caches, win, rel_bias, cmp_w_k, cmp_w_v, past)
    xs = _out_proj(xs, o_sb_s, o_nsa_s, sb_out_g[l], nsa_out_g[l], w_o_b, mix_post_g[l], tm=tm_s)
    xs = _ffn(xs, *ffn2, tm=tm_s)

    sbh = lambda a, n: a.reshape(1, -1, n, SB_HEADS, HEAD_DIM)
    kvh = lambda a, n: a.reshape(1, -1, n, NSA_KV_HEADS, HEAD_DIM)
    prompt_kv = [kvh(pf[name], s_len) for name in ("ck", "cv", "sk2", "sv2")]
    sample_kv = [kvh(sf[name], t_new) for name in ("ck", "cv", "sk2", "sv2")]
    new_wk = jnp.concatenate([state_win_k[l].reshape(nb_s, -1, KV_W), sf["wk"].reshape(nb_s, t_new, KV_W)],
                             axis=1)[:, t_new:]
    new_wv = jnp.concatenate([state_win_v[l].reshape(nb_s, -1, KV_W), sf["wv"].reshape(nb_s, t_new, KV_W)],
                             axis=1)[:, t_new:]
    return (xp[None], xs.reshape(nb_s, t_new, d_model),
            sbh(pf["sk"], s_len), sbh(pf["sv"], s_len), *prompt_kv,
            kvh(pf["wk"][s_len - w_keep:], w_keep), kvh(pf["wv"][s_len - w_keep:], w_keep),
            sbh(sf["sk"], t_new), sbh(sf["sv"], t_new), *sample_kv,
            kvh(new_wk, new_wk.shape[1]), kvh(new_wv, new_wv.shape[1]))
```

```python
import functools
import math

import numpy as np
import jax
import jax.numpy as jnp
from jax import lax
from jax.experimental import pallas as pl
from jax.experimental.pallas import tpu as pltpu

F32 = jnp.float32
BF16 = jnp.bfloat16

HEAD_DIM = 64
SB_HEADS = 8
NSA_HEADS = 8
NSA_KV_HEADS = 2
NSA_GROUP = 4
CMP_BLOCK = 32
CMP_STRIDE = 16
SEL_BLOCK = 64
SEL_TOPK = 16
SEL_PER_CMP = SEL_BLOCK // CMP_STRIDE
WINDOW = 512
Q_BLOCK = 128
NUM_BUCKETS = 32
MAX_EXACT = 16
MAX_DISTANCE = 1024
RMS_EPS = 1e-6
PAGE_SIZE = 128
SCALE = HEAD_DIM ** -0.5
LOG2E = math.log2(math.e)

LANES = 128
VMEM_LIMIT = 56 * 1024 * 1024
NEG = -1e30
EXP_ZERO = -104.0


def _bucket_thresholds():
    n = np.arange(0, 4 * MAX_DISTANCE)
    nf = np.maximum(n, 1).astype(np.float64)
    large = MAX_EXACT + (np.log(nf / MAX_EXACT) / math.log(MAX_DISTANCE / MAX_EXACT)
                         * (NUM_BUCKETS - MAX_EXACT)).astype(np.int64)
    bucket = np.where(n < MAX_EXACT, n, np.minimum(large, NUM_BUCKETS - 1))
    return [int(np.argmax(bucket >= b)) for b in range(1, NUM_BUCKETS)]


BUCKET_THR = _bucket_thresholds()
FAR_DIST = BUCKET_THR[-1]


def _cparams(sem):
    return pltpu.CompilerParams(dimension_semantics=sem, vmem_limit_bytes=VMEM_LIMIT)


def _dot(a, b):
    return jnp.dot(a, b, preferred_element_type=F32)


def _dot_nt(a, b):
    return lax.dot_general(a, b, (((1,), (1,)), ((), ())), preferred_element_type=F32)


def _dot_tn(a, b):
    return lax.dot_general(a, b, (((0,), (0,)), ((), ())), preferred_element_type=F32)


def _split_dot(x, w):
    hi = x.astype(BF16)
    lo = (x - hi.astype(F32)).astype(BF16)
    return _dot(hi, w) + _dot(lo, w)


def _split_dot_left(w, x):
    hi = x.astype(BF16)
    lo = (x - hi.astype(F32)).astype(BF16)
    return _dot(w, hi) + _dot(w, lo)


def _rms(x, g):
    return x * lax.rsqrt(jnp.mean(x * x, axis=-1, keepdims=True) + RMS_EPS) * g


def _ffn_kernel(x_ref, pre_ref, post_ref, wg_ref, wu_ref, wd_ref, o_ref, acc_ref, *, fc):
    x = x_ref[...]
    hb = _rms(x, pre_ref[...]).astype(BF16)
    n_f = wg_ref.shape[1]
    for c in range(n_f // fc):
        g = _dot(hb, wg_ref[:, c * fc:(c + 1) * fc])
        u = _dot(hb, wu_ref[:, c * fc:(c + 1) * fc])
        a = (jax.nn.silu(g) * u).astype(BF16)
        part = _dot(a, wd_ref[c * fc:(c + 1) * fc, :])
        if c == 0:
            acc_ref[...] = part
        else:
            acc_ref[...] += part
    o_ref[...] = x + 0.5 * _rms(acc_ref[...], post_ref[...])


def _ffn(x, pre_g, post_g, wg, wu, wd, *, tm):
    t, d = x.shape
    f = wg.shape[1]
    fc = 256
    const = lambda i: (0, 0)
    return pl.pallas_call(
        functools.partial(_ffn_kernel, fc=fc),
        grid=(t // tm,),
        in_specs=[
            pl.BlockSpec((tm, d), lambda i: (i, 0)),
            pl.BlockSpec((1, d), const),
            pl.BlockSpec((1, d), const),
            pl.BlockSpec((d, f), const, pipeline_mode=pl.Buffered(1)),
            pl.BlockSpec((d, f), const, pipeline_mode=pl.Buffered(1)),
            pl.BlockSpec((f, d), const, pipeline_mode=pl.Buffered(1)),
        ],
        out_specs=pl.BlockSpec((tm, d), lambda i: (i, 0)),
        out_shape=jax.ShapeDtypeStruct((t, d), F32),
        scratch_shapes=[pltpu.VMEM((tm, d), F32)],
        compiler_params=_cparams(("parallel",)),
        name="ffn_half",
    )(x, pre_g.reshape(1, d), post_g.reshape(1, d), wg, wu, wd)


SB_W = SB_HEADS * HEAD_DIM
NSA_W = NSA_HEADS * HEAD_DIM
KV_W = NSA_KV_HEADS * HEAD_DIM
GATE_W = 3 * NSA_HEADS
IN_PIECES = (("sq", SB_W), ("sk", SB_W), ("sv", SB_W), ("nq", NSA_W),
             ("ck", KV_W), ("cv", KV_W), ("sk2", KV_W), ("sv2", KV_W),
             ("wk", KV_W), ("wv", KV_W), ("gt", LANES))
IN_BF16 = ("sq", "sk", "sv", "nq", "sk2", "sv2", "wk", "wv")


def _in_proj_kernel(x_ref, g_ref, w_ref, *out_refs):
    hb = _rms(x_ref[...], g_ref[...]).astype(BF16)
    f32_refs = out_refs[:len(IN_PIECES)]
    bf_refs = dict(zip(IN_BF16, out_refs[len(IN_PIECES):]))
    off = 0
    for (name, width), o_ref in zip(IN_PIECES, f32_refs):
        p = _dot(hb, w_ref[:, off:off + width])
        o_ref[...] = p
        if name in bf_refs:
            bf_refs[name][...] = p.astype(BF16)
        off += width


def _in_proj(x, pre_g, w_in_pad, *, tm):
    t, d = x.shape
    wp = w_in_pad.shape[1]
    row = lambda i: (i, 0)
    const = lambda i: (0, 0)
    widths = dict(IN_PIECES)
    out_shape = ([jax.ShapeDtypeStruct((t, w), F32) for _, w in IN_PIECES]
                 + [jax.ShapeDtypeStruct((t, widths[n]), BF16) for n in IN_BF16])
    out_specs = ([pl.BlockSpec((tm, w), row) for _, w in IN_PIECES]
                 + [pl.BlockSpec((tm, widths[n]), row) for n in IN_BF16])
    outs = pl.pallas_call(
        _in_proj_kernel,
        grid=(t // tm,),
        in_specs=[pl.BlockSpec((tm, d), row),
                  pl.BlockSpec((1, d), const),
                  pl.BlockSpec((d, wp), const, pipeline_mode=pl.Buffered(1))],
        out_specs=out_specs,
        out_shape=out_shape,
        compiler_params=_cparams(("parallel",)),
        name="in_proj",
    )(x, pre_g.reshape(1, d), w_in_pad)
    f32 = {n: o for (n, _), o in zip(IN_PIECES, outs[:len(IN_PIECES)])}
    bf = dict(zip(IN_BF16, outs[len(IN_PIECES):]))
    return f32, bf


def _out_proj_kernel(x_ref, osb_ref, onsa_ref, sbg_ref, nsag_ref, wo_ref, post_ref, o_ref):
    a = _rms(osb_ref[...], sbg_ref[...]).astype(BF16)
    b = _rms(onsa_ref[...], nsag_ref[...]).astype(BF16)
    y = _dot(a, wo_ref[:SB_W, :]) + _dot(b, wo_ref[SB_W:, :])
    o_ref[...] = x_ref[...] + _rms(y, post_ref[...])


def _out_proj(x, o_sb, o_nsa, sb_g, nsa_g, w_o, post_g, *, tm):
    t, d = x.shape
    row = lambda i: (i, 0)
    const = lambda i: (0, 0)
    return pl.pallas_call(
        _out_proj_kernel,
        grid=(t // tm,),
        in_specs=[pl.BlockSpec((tm, d), row),
                  pl.BlockSpec((tm, SB_W), row),
                  pl.BlockSpec((tm, NSA_W), row),
                  pl.BlockSpec((1, SB_W), const),
                  pl.BlockSpec((1, NSA_W), const),
                  pl.BlockSpec((SB_W + NSA_W, d), const, pipeline_mode=pl.Buffered(1)),
                  pl.BlockSpec((1, d), const)],
        out_specs=pl.BlockSpec((tm, d), row),
        out_shape=jax.ShapeDtypeStruct((t, d), F32),
        compiler_params=_cparams(("parallel",)),
        name="out_proj",
    )(x, o_sb, o_nsa, sb_g.reshape(1, SB_W), nsa_g.reshape(1, NSA_W), w_o, post_g.reshape(1, d))


def _sb_tri():
    r = lax.broadcasted_iota(jnp.int32, (Q_BLOCK, 2 * Q_BLOCK), 0)
    c = lax.broadcasted_iota(jnp.int32, (Q_BLOCK, 2 * Q_BLOCK), 1)
    return jnp.where((c >= Q_BLOCK) | (r > c), 1.0, 0.0).astype(BF16)


SB_TILE = 2 * Q_BLOCK
SB_HEADS_PER_STEP = 4
SB_LANES = SB_HEADS_PER_STEP * HEAD_DIM


def _log_sigmoid_neg(z):
    return -(jnp.maximum(z, 0.0) + jnp.log(1.0 + jnp.exp(-jnp.abs(z))))


def _sb_prompt_kernel(q_ref, k_ref, v_ref, o_ref, ls_ref, acc_ref):
    i = pl.program_id(1)
    r = lax.broadcasted_iota(jnp.int32, (SB_TILE, SB_TILE + Q_BLOCK), 0)
    c = lax.broadcasted_iota(jnp.int32, (SB_TILE, SB_TILE + Q_BLOCK), 1)
    tri = jnp.where((c >= SB_TILE) | (r > c), 1.0, 0.0).astype(BF16)
    q_pos = Q_BLOCK * i + lax.broadcasted_iota(jnp.int32, (Q_BLOCK, SB_TILE), 0)
    lane = lax.broadcasted_iota(jnp.int32, (Q_BLOCK, SB_TILE), 1)
    ls_ref[...] = jnp.zeros_like(ls_ref)
    acc_ref[...] = jnp.zeros_like(acc_ref)

    def tile(tt):
        hi = (i - 2 * tt + 1) * Q_BLOCK
        start = pl.multiple_of(jnp.maximum(hi - SB_TILE, 0), Q_BLOCK)
        k_pos = start + lane
        valid = (k_pos < q_pos) & (k_pos < hi)
        top = jnp.float32(NEG)
        for hh in range(SB_HEADS_PER_STEP):
            lanes = slice(hh * HEAD_DIM, (hh + 1) * HEAD_DIM)
            z = _dot_nt(q_ref[:, lanes], k_ref[pl.ds(start, SB_TILE), lanes]) * SCALE
            ln = _log_sigmoid_neg(z)
            st = _split_dot(jnp.where(valid, ln, 0.0), tri)
            ls = ls_ref[hh]
            a = jnp.exp(z + ln + st[:, :SB_TILE] + jnp.concatenate([ls, ls], axis=1))
            a = jnp.where(valid, a, 0.0)
            acc_ref[hh] += _dot(a.astype(BF16), v_ref[pl.ds(start, SB_TILE), lanes])
            ls = ls + st[:, SB_TILE:]
            ls_ref[hh] = ls
            top = jnp.maximum(top, jnp.max(ls))
        return top

    def cond(carry):
        tt, top = carry
        return (i - 2 * tt + 1 > 0) & (top > EXP_ZERO)

    def body(carry):
        tt, _ = carry
        return tt + 1, tile(tt)

    lax.while_loop(cond, body, (1, tile(0)))
    for hh in range(SB_HEADS_PER_STEP):
        o_ref[:, hh * HEAD_DIM:(hh + 1) * HEAD_DIM] = acc_ref[hh]


def _sb_prompt(q, k, v):
    t = q.shape[0]
    assert t >= SB_TILE
    return pl.pallas_call(
        _sb_prompt_kernel,
        grid=(SB_W // SB_LANES, t // Q_BLOCK),
        in_specs=[pl.BlockSpec((Q_BLOCK, SB_LANES), lambda h, i: (i, h)),
                  pl.BlockSpec((t, SB_LANES), lambda h, i: (0, h)),
                  pl.BlockSpec((t, SB_LANES), lambda h, i: (0, h))],
        out_specs=pl.BlockSpec((Q_BLOCK, SB_LANES), lambda h, i: (i, h)),
        out_shape=jax.ShapeDtypeStruct((t, SB_W), F32),
        scratch_shapes=[pltpu.VMEM((SB_HEADS_PER_STEP, Q_BLOCK, Q_BLOCK), F32),
                        pltpu.VMEM((SB_HEADS_PER_STEP, Q_BLOCK, HEAD_DIM), F32)],
        compiler_params=_cparams(("parallel", "parallel")),
        name="sb_prompt",
    )(q, k, v)


def _pad_w_in(w_in):
    wp = sum(w for _, w in IN_PIECES)
    return jnp.pad(w_in, ((0, 0), (0, wp - w_in.shape[1]))).astype(BF16)


def _bias_from_dist(dist, rb_at):
    out = jnp.where(dist >= BUCKET_THR[0], rb_at(1), rb_at(0))
    for b in range(2, NUM_BUCKETS):
        out = jnp.where(dist >= BUCKET_THR[b - 1], rb_at(b), out)
    return out


CMP_CHUNK = 128
CMP_CHUNK_SPAN = CMP_CHUNK * CMP_STRIDE
QB_PER_CHUNK = CMP_CHUNK_SPAN // Q_BLOCK
NEAR_TILES = -(-(FAR_DIST + Q_BLOCK - 1) // Q_BLOCK)
FAR_TILE = NEAR_TILES
WIN_TAIL_TILE = NEAR_TILES + 1
DEAD_TILE = NEAR_TILES + 2
BIAS_TILES = NEAR_TILES + 3
WIN_BLOCKS = WINDOW // Q_BLOCK


def _bias_prompt_kernel(rb_ref, bt_ref, bct_ref):
    h = pl.program_id(0)
    row = lax.broadcasted_iota(jnp.int32, (Q_BLOCK, Q_BLOCK), 0)
    col = lax.broadcasted_iota(jnp.int32, (Q_BLOCK, Q_BLOCK), 1)
    rb_at = lambda b: rb_ref[b, h]
    far = rb_ref[NUM_BUCKETS - 1, h]

    def tile(d, carry):
        bias = (_bias_from_dist(Q_BLOCK * d + col - row, rb_at) - far) * LOG2E
        bt_ref[d] = jnp.where((d > 0) | (row <= col), bias, NEG)
        return carry

    lax.fori_loop(0, FAR_TILE + 1, tile, 0)
    tail = (_bias_from_dist(Q_BLOCK * WIN_BLOCKS + col - row, rb_at) - far) * LOG2E
    bt_ref[WIN_TAIL_TILE] = jnp.where(row > col, tail, NEG)
    bt_ref[DEAD_TILE] = jnp.full((Q_BLOCK, Q_BLOCK), NEG, F32)

    def ctile(rw, carry):
        r = rw // 2
        w = rw % 2
        dist = (Q_BLOCK * r + col - CMP_STRIDE * row - (CMP_BLOCK - 1)
                + CMP_CHUNK_SPAN * (1 - w))
        bct_ref[r, w] = _bias_from_dist(dist, rb_at)
        return carry

    lax.fori_loop(0, 2 * QB_PER_CHUNK, ctile, 0)


def _bias_prompt(rel_bias):
    nh = rel_bias.shape[1]
    return pl.pallas_call(
        _bias_prompt_kernel,
        grid=(nh,),
        in_specs=[pl.BlockSpec(memory_space=pltpu.SMEM)],
        out_specs=[pl.BlockSpec((None, BIAS_TILES, Q_BLOCK, Q_BLOCK), lambda h: (h, 0, 0, 0)),
                   pl.BlockSpec((None, QB_PER_CHUNK, 2, CMP_CHUNK, Q_BLOCK), lambda h: (h, 0, 0, 0, 0))],
        out_shape=[jax.ShapeDtypeStruct((nh, BIAS_TILES, Q_BLOCK, Q_BLOCK), F32),
                   jax.ShapeDtypeStruct((nh, QB_PER_CHUNK, 2, CMP_CHUNK, Q_BLOCK), F32)],
        compiler_params=_cparams(("parallel",)),
        name="bias_prompt",
    )(rel_bias)


GROUP_W = CMP_STRIDE * KV_W


def _compress_weights(w1, w2, pos):
    eye = jnp.eye(NSA_KV_HEADS, dtype=w1.dtype)
    halves = []
    for half in range(2):
        wh = w1[half * CMP_STRIDE:(half + 1) * CMP_STRIDE]
        halves.append(jnp.einsum("lde,pk->lpdke", wh, eye).reshape(GROUP_W, KV_W))
    w_big = jnp.concatenate(halves, axis=1).astype(BF16)
    w_pos = jnp.tile(w1.reshape(CMP_BLOCK * HEAD_DIM, HEAD_DIM), (1, NSA_KV_HEADS)).astype(BF16)
    pos_row = jnp.pad(pos.reshape(1, CMP_BLOCK * HEAD_DIM), ((0, 7), (0, 0))).astype(BF16)
    w2_bd = jnp.einsum("ef,pk->pekf", w2, eye).reshape(KV_W, KV_W).astype(BF16)
    return w_big, _pos_term(pos_row, w_pos), w2_bd


def _pos_term_kernel(pos_ref, w_ref, o_ref):
    o_ref[...] = _dot(pos_ref[...], w_ref[...])


def _pos_term(pos_row, w_pos):
    return pl.pallas_call(
        _pos_term_kernel,
        out_shape=jax.ShapeDtypeStruct((pos_row.shape[0], KV_W), F32),
        name="pos_term",
    )(pos_row, w_pos)


def _compress_finish(ab_ref, n_rows, c_rows, w2_bd):
    h = ab_ref[0:n_rows, 0:KV_W] + ab_ref[1:n_rows + 1, KV_W:2 * KV_W] + c_rows[0:1, :]
    return _dot(jax.nn.silu(h).astype(BF16), w2_bd)


def _compress_kernel(g_ref, wbig_ref, c_ref, w2_ref, o_ref, ab_ref):
    n = g_ref.shape[0]
    ab_ref[0:n, :] = _dot(g_ref[...].astype(BF16), wbig_ref[...])
    ab_ref[n:n + 8, :] = jnp.zeros((8, 2 * KV_W), F32)
    o_ref[...] = _compress_finish(ab_ref, n, c_ref[...], w2_ref[...])


def _compress_prompt(rows, w_big, c_rows, w2_bd):
    g = rows.reshape(rows.shape[0] // CMP_STRIDE, GROUP_W)
    n = g.shape[0]
    return pl.pallas_call(
        _compress_kernel,
        out_shape=jax.ShapeDtypeStruct((n, KV_W), F32),
        scratch_shapes=[pltpu.VMEM((n + 8, 2 * KV_W), F32)],
        compiler_params=pltpu.CompilerParams(vmem_limit_bytes=VMEM_LIMIT),
        name="compress_prompt",
    )(g, w_big, c_rows, w2_bd)


BIG = 3e38
SEL_U = NEAR_TILES // 2
GQ = NSA_GROUP * Q_BLOCK


def _importance_matrices(n_half, n_cmp):
    r = np.arange(n_half)[:, None]
    n = np.arange(n_cmp)[None, :]
    mats = []
    for parity in range(2):
        j = 2 * r + parity
        mats.append(((n >= SEL_PER_CMP * j - 1) & (n <= SEL_PER_CMP * j + SEL_PER_CMP - 1)).astype(np.float32))
    return jnp.asarray(mats[0], BF16), jnp.asarray(mats[1], BF16)


def _topk_mask(imp_e, imp_o, cur, n_top):
    shape = imp_e.shape
    jr = lax.broadcasted_iota(jnp.int32, shape, 0)
    je = (2 * jr).astype(F32)
    jo = (2 * jr + 1).astype(F32)
    curf = cur.astype(F32)

    def encode(imp, j):
        valid = j <= curf
        forced = (j == 0.0) | (j == curf) | (j == curf - 1.0)
        return jnp.where(valid, jnp.where(forced, BIG, imp), -1.0), valid

    se, valid_e = encode(imp_e, je)
    so, valid_o = encode(imp_o, jo)
    sel_e = jnp.zeros(shape, F32)
    sel_o = jnp.zeros(shape, F32)
    for _ in range(n_top):
        mx = jnp.maximum(jnp.max(se, axis=0, keepdims=True), jnp.max(so, axis=0, keepdims=True))
        cand = jnp.minimum(jnp.min(jnp.where(se == mx, je, BIG), axis=0, keepdims=True),
                           jnp.min(jnp.where(so == mx, jo, BIG), axis=0, keepdims=True))
        hit_e = je == cand
        hit_o = jo == cand
        sel_e = jnp.where(hit_e, 1.0, sel_e)
        sel_o = jnp.where(hit_o, 1.0, sel_o)
        se = jnp.where(hit_e, -2.0, se)
        so = jnp.where(hit_o, -2.0, so)
    return jnp.where(valid_e, sel_e, 0.0), jnp.where(valid_o, sel_o, 0.0)


def _nsa_prompt_kernel(rb_ref, q_ref, ck_ref, cvT_ref, sk_ref, svT_ref, wk_ref, wvT_ref, gtT_ref,
                       bt_ref, bct_ref, ae_ref, ao_ref, o_ref, sc_ref, sele_ref, selo_ref,
                       sa_ref, sb_ref, pa_ref, pb_ref, *, n_top):
    k = pl.program_id(0)
    i = pl.program_id(1)
    n_grp = ck_ref.shape[0]
    q = jnp.concatenate(
        [(q_ref[:, g * HEAD_DIM:(g + 1) * HEAD_DIM].astype(F32) * SCALE).astype(BF16)
         for g in range(NSA_GROUP)], axis=0)
    rb_far = [rb_ref[NUM_BUCKETS - 1, NSA_GROUP * k + g] for g in range(NSA_GROUP)]
    row = lax.broadcasted_iota(jnp.int32, (Q_BLOCK, Q_BLOCK), 0)
    col = lax.broadcasted_iota(jnp.int32, (Q_BLOCK, Q_BLOCK), 1)
    q_pos = Q_BLOCK * i + col
    gsl = lambda g: slice(g * Q_BLOCK, (g + 1) * Q_BLOCK)
    diag = i // QB_PER_CHUNK

    def cmp_scores(c, dst, near_w, live):
        cs = pl.multiple_of(c * CMP_CHUNK, CMP_CHUNK)
        s = _dot_nt(ck_ref[pl.ds(cs, CMP_CHUNK), :], q)
        if near_w is not None:
            valid = (CMP_STRIDE * (cs + row) + (CMP_BLOCK - 1) <= q_pos) & live
        parts = []
        for g in range(NSA_GROUP):
            if near_w is None:
                parts.append(s[:, gsl(g)] + rb_far[g])
            else:
                parts.append(jnp.where(valid, s[:, gsl(g)] + bct_ref[g, 0, near_w], NEG))
        s = jnp.concatenate(parts, axis=1)
        sc_ref[pl.ds(pl.multiple_of(dst, CMP_CHUNK), CMP_CHUNK), :] = s
        return jnp.max(s, axis=0, keepdims=True)

    m_c = lax.fori_loop(0, jnp.maximum(diag - 1, 0),
                        lambda c, m: jnp.maximum(m, cmp_scores(c, c * CMP_CHUNK, None, None)),
                        jnp.full((1, GQ), NEG, F32))
    has_prev = diag >= 1
    m_c = jnp.maximum(m_c, cmp_scores(jnp.maximum(diag - 1, 0),
                                      jnp.where(has_prev, (diag - 1) * CMP_CHUNK, n_grp), 0, has_prev))
    m_c = jnp.maximum(m_c, cmp_scores(diag, diag * CMP_CHUNK, 1, True))
    m_c = jnp.where(m_c > 0.5 * NEG, m_c, 0.0)

    def cmp_sum(c, l):
        cs = pl.multiple_of(c * CMP_CHUNK, CMP_CHUNK)
        return l + jnp.sum(jnp.exp(sc_ref[pl.ds(cs, CMP_CHUNK), :] - m_c), axis=0, keepdims=True)

    l_c = lax.fori_loop(0, diag + 1, cmp_sum, jnp.zeros((1, GQ), F32))
    r_c = 1.0 / jnp.maximum(l_c, 1e-30)

    n_half = sele_ref.shape[0]

    def cmp_out(c, carry):
        oc, ie, io = carry
        cs = pl.multiple_of(c * CMP_CHUNK, CMP_CHUNK)
        p = jnp.exp(sc_ref[pl.ds(cs, CMP_CHUNK), :] - m_c) * r_c
        oc = oc + _dot(cvT_ref[:, pl.ds(cs, CMP_CHUNK)], p.astype(BF16))
        pg = p[:, gsl(0)] + p[:, gsl(1)] + p[:, gsl(2)] + p[:, gsl(3)]
        ie = ie + _split_dot_left(ae_ref[:, pl.ds(cs, CMP_CHUNK)], pg)
        io = io + _split_dot_left(ao_ref[:, pl.ds(cs, CMP_CHUNK)], pg)
        return oc, ie, io

    o_c, imp_e, imp_o = lax.fori_loop(
        0, diag + 1, cmp_out,
        (jnp.zeros((HEAD_DIM, GQ), F32), jnp.zeros((n_half, Q_BLOCK), F32), jnp.zeros((n_half, Q_BLOCK), F32)))

    cur = (Q_BLOCK * i + lax.broadcasted_iota(jnp.int32, (n_half, Q_BLOCK), 1)) // SEL_BLOCK
    sel_e, sel_o = _topk_mask(imp_e, imp_o, cur, n_top)
    sele_ref[...] = sel_e
    selo_ref[...] = sel_o

    q2 = jnp.concatenate(
        [(q_ref[:, g * HEAD_DIM:(g + 1) * HEAD_DIM].astype(F32) * (SCALE * LOG2E)).astype(BF16)
         for g in range(NSA_GROUP)], axis=0)
    def scores(keys_ref, d):
        cs = pl.multiple_of(jnp.maximum(i - d, 0) * Q_BLOCK, Q_BLOCK)
        return _dot_nt(keys_ref[pl.ds(cs, Q_BLOCK), :], q2)

    def values(valsT_ref, d):
        cs = pl.multiple_of(jnp.maximum(i - d, 0) * Q_BLOCK, Q_BLOCK)
        return valsT_ref[:, pl.ds(cs, Q_BLOCK)]

    def biased(s, tile, extra):
        parts = []
        for g in range(NSA_GROUP):
            if tile is None:
                add = extra
            else:
                add = bt_ref[g, tile] if extra is None else bt_ref[g, tile] + extra
            parts.append(s[:, gsl(g)] + add)
        return jnp.concatenate(parts, axis=1)

    def update(s_list, m, acc):
        m_new = m
        for s in s_list:
            m_new = jnp.maximum(m_new, jnp.max(s, axis=0, keepdims=True))
        return [jnp.exp2(s - m_new).astype(BF16) for s in s_list], m_new, jnp.exp2(m - m_new) * acc

    def tile_of(d, last):
        return jnp.where(d <= i, jnp.minimum(d, last), DEAD_TILE)

    def sel_mask(d, guard):
        c2 = jnp.maximum(i - d, 0)

        def off(r):
            r = jnp.where(r > 0.0, 0.0, NEG)
            if guard:
                r = jnp.where(d <= i, r, NEG)
            return jnp.broadcast_to(r, (SEL_BLOCK, Q_BLOCK))

        return jnp.concatenate([off(sele_ref[pl.ds(c2, 1), :]), off(selo_ref[pl.ds(c2, 1), :])], axis=0)

    acc_rows = svT_ref.shape[0]
    m0 = jnp.full((1, GQ), NEG, F32)
    acc0 = jnp.zeros((acc_rows, GQ), F32)

    def sel_trip(d0, far, s_cur, p_cur, s_nxt, p_prv, m, acc):
        for u in range(SEL_U):
            s_nxt[u] = scores(sk_ref, d0 + SEL_U + u)
        pv = _dot(values(svT_ref, jnp.maximum(d0 - SEL_U, 0)), p_prv[0])
        for u in range(1, SEL_U):
            pv = pv + _dot(values(svT_ref, jnp.maximum(d0 - SEL_U + u, 0)), p_prv[u])
        tile = (lambda d: None) if far else (lambda d: tile_of(d, NEAR_TILES - 1))
        sb = [biased(s_cur[u], tile(d0 + u), sel_mask(d0 + u, far)) for u in range(SEL_U)]
        p, m, acc = update(sb, m, acc + pv)
        for u in range(SEL_U):
            p_cur[u] = p[u]
        return m, acc

    def sel_round(j, far, m, acc):
        m, acc = sel_trip(2 * SEL_U * j, far, sa_ref, pa_ref, sb_ref, pb_ref, m, acc)
        return sel_trip(2 * SEL_U * j + SEL_U, far, sb_ref, pb_ref, sa_ref, pa_ref, m, acc)

    for u in range(SEL_U):
        sa_ref[u] = scores(sk_ref, u)
    pb_ref[...] = jnp.zeros(pb_ref.shape, BF16)
    n_trips = i // SEL_U + 1
    n_rounds = (n_trips + 1) // 2
    m, acc = sel_round(0, False, m0, acc0)
    _, o_s = lax.fori_loop(1, n_rounds, lambda j, c: sel_round(j, True, *c), (m, acc))
    last = 2 * SEL_U * n_rounds - SEL_U
    for u in range(SEL_U):
        o_s = o_s + _dot(values(svT_ref, last + u), pb_ref[u])

    s_w = [scores(wk_ref, d) for d in range(WIN_BLOCKS + 1)]
    m, o_w = m0, acc0
    for d in range(WIN_BLOCKS + 1):
        tile = tile_of(d, NEAR_TILES - 1) if d < WIN_BLOCKS else jnp.where(d <= i, WIN_TAIL_TILE, DEAD_TILE)
        (p,), m, o_w = update([biased(s_w[d], tile, None)], m, o_w)
        o_w = o_w + _dot(values(wvT_ref, d), p)

    r_s = 1.0 / jnp.maximum(o_s[HEAD_DIM:HEAD_DIM + 1, :], 1e-30)
    r_w = 1.0 / jnp.maximum(o_w[HEAD_DIM:HEAD_DIM + 1, :], 1e-30)
    for g in range(NSA_GROUP):
        gate = lambda br: jax.nn.sigmoid(gtT_ref[pl.ds(br * NSA_HEADS + NSA_GROUP * k + g, 1), :])
        o = (gate(0) * o_c[:, gsl(g)]
             + gate(1) * r_s[:, gsl(g)] * o_s[:HEAD_DIM, gsl(g)]
             + gate(2) * r_w[:, gsl(g)] * o_w[:HEAD_DIM, gsl(g)])
        o_ref[:, g * HEAD_DIM:(g + 1) * HEAD_DIM] = o.T


def _nsa_prompt(rel_bias, nq, ck, cvT, sk, svT, wk, wvT, gtT, bt, bct):
    t = nq.shape[0]
    n_grp = ck.shape[1]
    n_half = t // (2 * SEL_BLOCK)
    n_top = min(SEL_TOPK, t // SEL_BLOCK)
    a_e, a_o = _importance_matrices(n_half, n_grp)
    per_k3 = lambda shape: pl.BlockSpec((None,) + shape, lambda k, i: (k, 0, 0))
    const2 = lambda shape: pl.BlockSpec(shape, lambda k, i: (0, 0))
    return pl.pallas_call(
        functools.partial(_nsa_prompt_kernel, n_top=n_top),
        grid=(NSA_KV_HEADS, t // Q_BLOCK),
        in_specs=[pl.BlockSpec(memory_space=pltpu.SMEM),
                  pl.BlockSpec((Q_BLOCK, NSA_GROUP * HEAD_DIM), lambda k, i: (i, k)),
                  per_k3((n_grp, HEAD_DIM)), per_k3((HEAD_DIM, n_grp)),
                  per_k3((t, HEAD_DIM)), per_k3((VAL_ROWS, t)),
                  per_k3((t, HEAD_DIM)), per_k3((VAL_ROWS, t)),
                  pl.BlockSpec((GATE_W, Q_BLOCK), lambda k, i: (0, i)),
                  pl.BlockSpec((NSA_GROUP, BIAS_TILES, Q_BLOCK, Q_BLOCK), lambda k, i: (k, 0, 0, 0)),
                  pl.BlockSpec((NSA_GROUP, 1, 2, CMP_CHUNK, Q_BLOCK),
                               lambda k, i: (k, i % QB_PER_CHUNK, 0, 0, 0)),
                  const2((n_half, n_grp)), const2((n_half, n_grp))],
        out_specs=pl.BlockSpec((Q_BLOCK, NSA_GROUP * HEAD_DIM), lambda k, i: (i, k)),
        out_shape=jax.ShapeDtypeStruct((t, NSA_W), F32),
        scratch_shapes=[pltpu.VMEM((n_grp + CMP_CHUNK, GQ), F32),
                        pltpu.VMEM((n_half, Q_BLOCK), F32),
                        pltpu.VMEM((n_half, Q_BLOCK), F32),
                        pltpu.VMEM((SEL_U, Q_BLOCK, GQ), F32),
                        pltpu.VMEM((SEL_U, Q_BLOCK, GQ), F32),
                        pltpu.VMEM((SEL_U, Q_BLOCK, GQ), BF16),
                        pltpu.VMEM((SEL_U, Q_BLOCK, GQ), BF16)],
        compiler_params=_cparams(("parallel", "arbitrary")),
        name="nsa_prompt",
    )(rel_bias, nq, ck, cvT, sk, svT, wk, wvT, gtT, bt, bct, a_e, a_o)


def _per_kv_head(x):
    t = x.shape[0]
    xk = x.reshape(t, NSA_KV_HEADS, HEAD_DIM).transpose(1, 0, 2)
    return xk, xk.transpose(0, 2, 1)


VAL_ROWS = HEAD_DIM + 8


def _with_ones_row(v_t):
    heads, _, t = v_t.shape
    extra = jnp.zeros((heads, VAL_ROWS - HEAD_DIM, t), v_t.dtype).at[:, 0, :].set(1)
    return jnp.concatenate([v_t, extra], axis=1)


def _nsa_prompt_group(rel_bias, f32, bf, cmp_w_k, cmp_w_v):
    ck = _compress_prompt(f32["ck"], *cmp_w_k)
    cv = _compress_prompt(f32["cv"], *cmp_w_v)
    ck_h, _ = _per_kv_head(ck.astype(BF16))
    _, cv_t = _per_kv_head(cv.astype(BF16))
    sk_h, _ = _per_kv_head(bf["sk2"])
    sv_t = _with_ones_row(_per_kv_head(bf["sv2"])[1])
    wk_h, _ = _per_kv_head(bf["wk"])
    wv_t = _with_ones_row(_per_kv_head(bf["wv"])[1])
    gt_t = f32["gt"][:, :GATE_W].T
    bt, bct = _bias_prompt(rel_bias)
    return _nsa_prompt(rel_bias, bf["nq"], ck_h, cv_t, sk_h, sv_t, wk_h, wv_t, gt_t, bt, bct)


PAGES_PER_STEP = 32
ROWS_S = NSA_HEADS * 4
KT = NSA_KV_HEADS * 4


def _paged(cache):
    n_pool, rows, heads, d = cache.shape
    return cache.transpose(0, 2, 3, 1).reshape(n_pool, heads * d, rows)


def _page_specs(n, width, page_of):
    return [pl.BlockSpec((None, width, PAGE_SIZE),
                         functools.partial(lambda b, p, pt, c: (page_of(b, p, c, pt), 0, 0), c=c))
            for c in range(n)]


def _bias_sample_kernel(rbr_ref, bsc_ref, bsw_ref, bss_ref, *, past, w_buf):
    t_of = lambda shape: lax.broadcasted_iota(jnp.int32, shape, 0) % 4
    lane = lambda shape: lax.broadcasted_iota(jnp.int32, shape, 1)
    rb_at = lambda b: rbr_ref[:, b:b + 1]
    shp = bsc_ref.shape
    bsc_ref[...] = _bias_from_dist(past + t_of(shp) - CMP_STRIDE * lane(shp) - (CMP_BLOCK - 1), rb_at)
    shp = bsw_ref.shape
    bsw_ref[...] = _bias_from_dist(w_buf + t_of(shp) - lane(shp), rb_at)
    shp = (ROWS_S, LANES)

    def chunk(c, carry):
        cs = pl.multiple_of(c * LANES, LANES)
        bss_ref[:, pl.ds(cs, LANES)] = _bias_from_dist(past + t_of(shp) - cs - lane(shp), rb_at)
        return carry

    lax.fori_loop(0, bss_ref.shape[1] // LANES, chunk, 0)


def _bias_sample(rbr, past, w_buf, n_cmp_pad, w_pad):
    return pl.pallas_call(
        functools.partial(_bias_sample_kernel, past=past, w_buf=w_buf),
        out_shape=[jax.ShapeDtypeStruct((ROWS_S, n_cmp_pad), F32),
                   jax.ShapeDtypeStruct((ROWS_S, w_pad), F32),
                   jax.ShapeDtypeStruct((ROWS_S, past + PAGE_SIZE), F32)],
        compiler_params=pltpu.CompilerParams(vmem_limit_bytes=VMEM_LIMIT),
        name="bias_sample",
    )(rbr)


def _masked_softmax_lanes(s, valid):
    s = jnp.where(valid, s, NEG)
    m = jnp.max(s, axis=-1, keepdims=True)
    m = jnp.where(m > 0.5 * NEG, m, 0.0)
    e = jnp.where(valid, jnp.exp(s - m), 0.0)
    return e / jnp.maximum(jnp.sum(e, axis=-1, keepdims=True), 1e-30)


def _own_kv_lanes(o, kv_of_row):
    return jnp.where(kv_of_row == 0, o[:, :HEAD_DIM], o[:, HEAD_DIM:])


def _topk_mask_rows(imp, cur, n_blocks, n_top):
    j = lax.broadcasted_iota(jnp.int32, imp.shape, 0).astype(F32)
    curf = cur.astype(F32)
    valid = (j <= curf) & (j < float(n_blocks))
    forced = (j == 0.0) | (j == curf) | (j == curf - 1.0)
    score = jnp.where(valid, jnp.where(forced, BIG, imp), -1.0)
    sel = jnp.zeros(imp.shape, F32)
    for _ in range(n_top):
        mx = jnp.max(score, axis=0, keepdims=True)
        cand = jnp.min(jnp.where(score == mx, j, BIG), axis=0, keepdims=True)
        hit = j == cand
        sel = jnp.where(hit, 1.0, sel)
        score = jnp.where(hit, -2.0, score)
    return jnp.where(valid, sel, 0.0)


def _nsa_sample_cmp_kernel(pt_ref, *refs, past, n_top, n_sel):
    del pt_ref
    n = PAGES_PER_STEP
    kp = refs[:n]
    vp = refs[n:2 * n]
    (q_ref, nk_ref, nv_ref, wsk_ref, wsv_ref, nwk_ref, nwv_ref,
     wbk_ref, ck_ref, w2k_ref, wbv_ref, cv_ref, w2v_ref,
     at_ref, bsc_ref, bsw_ref,
     oc_ref, ow_ref, sel_ref,
     abk_ref, abv_ref, new8_ref, xt_ref, nwk128_ref, nwv128_ref) = refs[2 * n:]
    p = pl.program_id(1)
    n_steps = pl.num_programs(1)
    groups_per_page = PAGE_SIZE // CMP_STRIDE
    rows_per_step = n * groups_per_page
    n_cmp = abk_ref.shape[0] - 8
    w_buf = wsk_ref.shape[1]

    base = pl.multiple_of(p * rows_per_step, rows_per_step)
    pr = lax.broadcasted_iota(jnp.int32, (PAGE_SIZE, PAGE_SIZE), 0)
    pc = lax.broadcasted_iota(jnp.int32, (PAGE_SIZE, PAGE_SIZE), 1)
    perm = (pc == CMP_STRIDE * (pr % groups_per_page) + pr // groups_per_page).astype(BF16)
    for pages, wb_ref, ab_ref in ((kp, wbk_ref, abk_ref), (vp, wbv_ref, abv_ref)):
        for c in range(0, n, 2):
            two = jnp.concatenate([pages[c][...], pages[c + 1][...]], axis=0).astype(BF16)
            xt = _dot_nt(perm, two)
            xt_ref[c] = xt[:, :KV_W]
            xt_ref[c + 1] = xt[:, KV_W:]
        ab = jnp.zeros((rows_per_step, 2 * KV_W), F32)
        for l in range(0, CMP_STRIDE, 2):
            x_l = xt_ref[:, l * groups_per_page:(l + 2) * groups_per_page, :]
            x_l = jnp.concatenate([x_l[:, :groups_per_page].reshape(rows_per_step, KV_W),
                                   x_l[:, groups_per_page:].reshape(rows_per_step, KV_W)], axis=1)
            ab = ab + _dot(x_l.astype(BF16), wb_ref[l * KV_W:(l + 2) * KV_W, :])
        ab_ref[pl.ds(base, rows_per_step), :] = ab

    @pl.when(p == n_steps - 1)
    def _():
        r8 = lax.broadcasted_iota(jnp.int32, (8, KV_W), 0)
        kv_of_row = (lax.broadcasted_iota(jnp.int32, (ROWS_S, 1), 0) % KT) // 4
        t_row = lax.broadcasted_iota(jnp.int32, (ROWS_S, 1), 0) % 4
        q = q_ref[...]

        comp = []
        for new_ref, wb_ref, ab_ref, c_ref, w2_ref in (
                (nk_ref, wbk_ref, abk_ref, ck_ref, w2k_ref),
                (nv_ref, wbv_ref, abv_ref, cv_ref, w2v_ref)):
            new8_ref[...] = jnp.zeros((8, KV_W), F32)
            new8_ref[0:4, :] = new_ref[...]
            x = new8_ref[...]
            tot = jnp.zeros((8, 2 * KV_W), F32)
            for l in range(4):
                xl = jnp.where(r8 == l, x, 0.0).astype(BF16)
                tot = tot + _dot(xl, wb_ref[l * KV_W:(l + 1) * KV_W, :])
            ab_ref[n_cmp:n_cmp + 8, :] = jnp.where(
                lax.broadcasted_iota(jnp.int32, (8, 2 * KV_W), 0) == 0,
                jnp.sum(tot, axis=0, keepdims=True), 0.0)
            comp.append(_compress_finish(ab_ref, n_cmp, c_ref[...], w2_ref[...]))
        ck, cv = comp

        lane_c = lax.broadcasted_iota(jnp.int32, (ROWS_S, n_cmp), 1)
        valid_c = past + t_row - CMP_STRIDE * lane_c - (CMP_BLOCK - 1) >= 0
        s_c = _dot_nt(q, ck.astype(BF16)) * SCALE + bsc_ref[...]
        p_c = _masked_softmax_lanes(s_c, valid_c)
        oc_ref[...] = _own_kv_lanes(_dot(p_c.astype(BF16), cv.astype(BF16)), kv_of_row)

        pg = p_c[0:KT] + p_c[KT:2 * KT] + p_c[2 * KT:3 * KT] + p_c[3 * KT:4 * KT]
        pg = jnp.concatenate([pg, jnp.zeros((LANES - KT, n_cmp), F32)], axis=0)
        pg_hi = pg.astype(BF16)
        pg_lo = (pg - pg_hi.astype(F32)).astype(BF16)
        imp = _dot_nt(at_ref[...], pg_hi) + _dot_nt(at_ref[...], pg_lo)
        cur = (past + lax.broadcasted_iota(jnp.int32, (1, LANES), 1) % 4) // SEL_BLOCK
        sel_ref[...] = _topk_mask_rows(imp, cur, n_sel, n_top)[:, :KT]

        for nw_ref, pad_ref in ((nwk_ref, nwk128_ref), (nwv_ref, nwv128_ref)):
            pad_ref[...] = jnp.zeros(pad_ref.shape, F32)
            pad_ref[0:4, :] = nw_ref[...]
        lane_w = lax.broadcasted_iota(jnp.int32, (ROWS_S, w_buf + LANES), 1)
        dist_w = w_buf + t_row - lane_w
        valid_w = (dist_w >= 0) & (dist_w < WINDOW) & (past - w_buf + lane_w >= 0)
        s_w = jnp.concatenate([_dot(q, wsk_ref[...].astype(BF16)),
                               _dot_nt(q, nwk128_ref[...].astype(BF16))], axis=1) * SCALE + bsw_ref[...]
        p_w = _masked_softmax_lanes(s_w, valid_w).astype(BF16)
        o_w = _dot_nt(p_w[:, :w_buf], wsv_ref[...].astype(BF16)) + _dot(p_w[:, w_buf:], nwv128_ref[...].astype(BF16))
        ow_ref[...] = _own_kv_lanes(o_w, kv_of_row)


def _nsa_sample_cmp(page_table, cache_k, cache_v, qn, new_k, new_v, win_k, win_v, new_wk, new_wv,
                    cmp_w_k, cmp_w_v, a_t, bsc, bsw, *, past):
    nb, n_pages = page_table.shape
    n = PAGES_PER_STEP
    n_steps = n_pages // n
    t_new = new_k.shape[1]
    assert past % CMP_STRIDE == 0 and t_new <= CMP_STRIDE
    n_cmp = past // CMP_STRIDE
    n_sel = -(-(past + t_new) // SEL_BLOCK)
    n_top = min(SEL_TOPK, n_sel)
    w_buf = win_k.shape[2]
    page_of = lambda b, p, c, pt: pt[b, p * n + c]
    per_b = lambda shape: pl.BlockSpec((None,) + shape, lambda b, p, pt: (b,) + (0,) * len(shape))
    const = lambda a: pl.BlockSpec(a.shape, lambda b, p, pt: (0,) * a.ndim)
    weights = list(cmp_w_k) + list(cmp_w_v)
    out_shape = [jax.ShapeDtypeStruct((nb, ROWS_S, HEAD_DIM), F32),
                 jax.ShapeDtypeStruct((nb, ROWS_S, HEAD_DIM), F32),
                 jax.ShapeDtypeStruct((nb, a_t.shape[0], KT), F32)]
    grid_spec = pltpu.PrefetchScalarGridSpec(
        num_scalar_prefetch=1,
        grid=(nb, n_steps),
        in_specs=(_page_specs(n, KV_W, page_of) + _page_specs(n, KV_W, page_of)
                  + [per_b((ROWS_S, KV_W)), per_b((t_new, KV_W)), per_b((t_new, KV_W)),
                     per_b((KV_W, w_buf)), per_b((KV_W, w_buf)), per_b((t_new, KV_W)), per_b((t_new, KV_W))]
                  + [const(w) for w in weights] + [const(a_t), const(bsc), const(bsw)]),
        out_specs=[per_b(s.shape[1:]) for s in out_shape],
        scratch_shapes=[pltpu.VMEM((n_cmp + 8, 2 * KV_W), F32),
                        pltpu.VMEM((n_cmp + 8, 2 * KV_W), F32),
                        pltpu.VMEM((8, KV_W), F32),
                        pltpu.VMEM((n, PAGE_SIZE, KV_W), F32),
                        pltpu.VMEM((LANES, KV_W), F32),
                        pltpu.VMEM((LANES, KV_W), F32)])
    return pl.pallas_call(
        functools.partial(_nsa_sample_cmp_kernel, past=past, n_top=n_top, n_sel=n_sel),
        grid_spec=grid_spec,
        out_shape=out_shape,
        compiler_params=_cparams(("parallel", "arbitrary")),
        name="nsa_sample_cmp",
    )(page_table, *([cache_k] * n), *([cache_v] * n), qn, new_k, new_v, win_k, win_v, new_wk, new_wv,
      *weights, a_t, bsc, bsw)


def _nsa_sample_sel_kernel(pt_ref, *refs, past):
    del pt_ref
    n = PAGES_PER_STEP
    kp = refs[:n]
    vp = refs[n:2 * n]
    (q_ref, mask_ref, nmask_ref, nk_ref, nv_ref, bss_ref, oc_ref, ow_ref, gt_ref,
     o_ref, m_ref, l_ref, acc_ref, newk_ref, newv_ref) = refs[2 * n:]
    p = pl.program_id(1)
    n_steps = pl.num_programs(1)
    q = q_ref[...]
    lane = lax.broadcasted_iota(jnp.int32, (ROWS_S, PAGE_SIZE), 1)
    first_half = lane < SEL_BLOCK

    @pl.when(p == 0)
    def _():
        m_ref[...] = jnp.full(m_ref.shape, NEG, F32)
        l_ref[...] = jnp.zeros(l_ref.shape, F32)
        acc_ref[...] = jnp.zeros(acc_ref.shape, F32)

    def scores(qk, mk, bias, extra_valid=None):
        s = qk * SCALE + bias
        on = jnp.where(first_half, mk[:, 0:1], mk[:, 1:2]) > 0.0
        if extra_valid is not None:
            on = on & extra_valid
        return jnp.where(on, s, NEG)

    def update(s_list, v_list, paged):
        m_old = m_ref[...]
        m_new = m_old
        for s in s_list:
            m_new = jnp.maximum(m_new, jnp.max(s, axis=1, keepdims=True))
        alpha = jnp.exp(m_old - m_new)
        l = alpha * l_ref[...]
        acc = alpha * acc_ref[...]
        for s, v in zip(s_list, v_list):
            e = jnp.exp(s - m_new)
            l = l + jnp.sum(e, axis=1, keepdims=True)
            pv = _dot_nt if paged else _dot
            acc = acc + pv(e.astype(BF16), v.astype(BF16))
        m_ref[...] = m_new
        l_ref[...] = l
        acc_ref[...] = acc

    s_list = []
    for c in range(n):
        ks = pl.multiple_of((p * n + c) * PAGE_SIZE, PAGE_SIZE)
        s_list.append(scores(_dot(q, kp[c][...].astype(BF16)), mask_ref[c], bss_ref[:, pl.ds(ks, PAGE_SIZE)]))
    update(s_list, [r[...] for r in vp], True)

    @pl.when(p == n_steps - 1)
    def _():
        for src, dst in ((nk_ref, newk_ref), (nv_ref, newv_ref)):
            dst[...] = jnp.zeros(dst.shape, F32)
            dst[0:4, :] = src[...]
        t_row = lax.broadcasted_iota(jnp.int32, (ROWS_S, PAGE_SIZE), 0) % 4
        s_new = scores(_dot_nt(q, newk_ref[...].astype(BF16)), nmask_ref[...],
                       bss_ref[:, past:past + PAGE_SIZE], lane <= t_row)
        update([s_new], [newv_ref[...]], False)
        kv_of_row = (lax.broadcasted_iota(jnp.int32, (ROWS_S, 1), 0) % KT) // 4
        o_s = _own_kv_lanes(acc_ref[...] / jnp.maximum(l_ref[...], 1e-30), kv_of_row)
        gate = jax.nn.sigmoid(gt_ref[...])
        o_ref[...] = gate[:, 0:1] * oc_ref[...] + gate[:, 1:2] * o_s + gate[:, 2:3] * ow_ref[...]


def _nsa_sample_sel(page_table, cache_k, cache_v, qn, mask_pages, new_k, new_v, bss, o_c, o_w, gates, *, past):
    nb, n_pages = page_table.shape
    n = PAGES_PER_STEP
    n_steps = n_pages // n
    t_new = new_k.shape[1]
    ck2, cv2 = cache_k, cache_v
    page_of = lambda b, p, c, pt: pt[b, p * n + c]
    per_b = lambda shape: pl.BlockSpec((None,) + shape, lambda b, p, pt: (b,) + (0,) * len(shape))
    grid_spec = pltpu.PrefetchScalarGridSpec(
        num_scalar_prefetch=1,
        grid=(nb, n_steps),
        in_specs=(_page_specs(n, KV_W, page_of) + _page_specs(n, KV_W, page_of)
                  + [per_b((ROWS_S, KV_W)),
                     pl.BlockSpec((None, n, ROWS_S, 2), lambda b, p, pt: (b, p, 0, 0)),
                     pl.BlockSpec((None, None, ROWS_S, 2), lambda b, p, pt: (b, n_pages, 0, 0)),
                     per_b((t_new, KV_W)), per_b((t_new, KV_W)),
                     pl.BlockSpec(bss.shape, lambda b, p, pt: (0, 0)),
                     per_b((ROWS_S, HEAD_DIM)), per_b((ROWS_S, HEAD_DIM)), per_b((ROWS_S, 3))]),
        out_specs=per_b((ROWS_S, HEAD_DIM)),
        scratch_shapes=[pltpu.VMEM((ROWS_S, 1), F32), pltpu.VMEM((ROWS_S, 1), F32),
                        pltpu.VMEM((ROWS_S, KV_W), F32),
                        pltpu.VMEM((PAGE_SIZE, KV_W), F32), pltpu.VMEM((PAGE_SIZE, KV_W), F32)])
    return pl.pallas_call(
        functools.partial(_nsa_sample_sel_kernel, past=past),
        grid_spec=grid_spec,
        out_shape=jax.ShapeDtypeStruct((nb, ROWS_S, HEAD_DIM), F32),
        compiler_params=_cparams(("parallel", "arbitrary")),
        name="nsa_sample_sel",
    )(page_table, *([ck2] * n), *([cv2] * n), qn, mask_pages, mask_pages, new_k, new_v, bss, o_c, o_w, gates)


SB_AHEAD = 2


def _sb_sample_kernel(pt_ref, q_ref, nk_ref, nv_ref, ck_hbm, cv_hbm, o_ref,
                      ls_ref, acc_ref, newk_ref, newv_ref, top_ref,
                      kbuf, vbuf, ksem, vsem, kx, vx, xsem):
    b = pl.program_id(0)
    nb = pl.num_programs(0)
    n_pages = pt_ref.shape[1]
    slot = b % 2
    q = q_ref[...]
    tri = _sb_tri()

    def ahead(seq, sl, j):
        page = pt_ref[seq, n_pages - 1 - j]
        return (pltpu.make_async_copy(ck_hbm.at[page], kbuf.at[sl, j], ksem.at[sl, j]),
                pltpu.make_async_copy(cv_hbm.at[page], vbuf.at[sl, j], vsem.at[sl, j]))

    def start_ahead(seq, sl):
        for j in range(SB_AHEAD):
            for cp in ahead(seq, sl, j):
                cp.start()

    @pl.when(b == 0)
    def _():
        start_ahead(0, 0)

    @pl.when(b + 1 < nb)
    def _():
        start_ahead(b + 1, 1 - slot)

    def block(keys, vals, valid):
        paged = valid is None
        qk = _dot if paged else _dot_nt
        pv = _dot_nt if paged else _dot
        z = qk(q, keys.astype(BF16)) * SCALE
        ln = _log_sigmoid_neg(z)
        ln_m = ln if valid is None else jnp.where(valid, ln, 0.0)
        st = _split_dot(ln_m, tri)
        a = jnp.exp(z + ln + st[:, :PAGE_SIZE] + ls_ref[...])
        if valid is not None:
            a = jnp.where(valid, a, 0.0)
        acc_ref[...] += pv(a.astype(BF16), vals.astype(BF16))
        ls_ref[...] += st[:, PAGE_SIZE:]
        top_ref[0] = jnp.max(ls_ref[...])

    ls_ref[...] = jnp.zeros(ls_ref.shape, F32)
    acc_ref[...] = jnp.zeros(acc_ref.shape, F32)
    for src, dst in ((nk_ref, newk_ref), (nv_ref, newv_ref)):
        dst[...] = jnp.zeros(dst.shape, F32)
        dst[0:4, :] = src[...]
    lane = lax.broadcasted_iota(jnp.int32, (ROWS_S, PAGE_SIZE), 1)
    t_row = lax.broadcasted_iota(jnp.int32, (ROWS_S, PAGE_SIZE), 0) % 4
    block(newk_ref[...], newv_ref[...], lane < t_row)

    for j in range(SB_AHEAD):
        for cp in ahead(b, slot, j):
            cp.wait()

        @pl.when(top_ref[0] > EXP_ZERO)
        def _():
            block(kbuf[slot, j], vbuf[slot, j], None)

    def more(carry):
        j, top = carry
        return (j < n_pages) & (top > EXP_ZERO)

    def fetch(carry):
        j, _ = carry
        page = pt_ref[b, n_pages - 1 - j]
        ck = pltpu.make_async_copy(ck_hbm.at[page], kx, xsem.at[0])
        cv = pltpu.make_async_copy(cv_hbm.at[page], vx, xsem.at[1])
        ck.start()
        cv.start()
        ck.wait()
        cv.wait()
        block(kx[...], vx[...], None)
        return j + 1, top_ref[0]

    lax.while_loop(more, fetch, (SB_AHEAD, top_ref[0]))

    acc = acc_ref[...]
    head_of_row = lax.broadcasted_iota(jnp.int32, (ROWS_S, 1), 0) // 4
    o = jnp.zeros((ROWS_S, HEAD_DIM), F32)
    for h in range(SB_HEADS):
        o = jnp.where(head_of_row == h, acc[:, h * HEAD_DIM:(h + 1) * HEAD_DIM], o)
    o_ref[...] = o


def _sb_sample(page_table, cache_k, cache_v, qs, new_k, new_v):
    nb, n_pages = page_table.shape
    assert n_pages >= SB_AHEAD
    t_new = new_k.shape[1]
    per_b = lambda shape: pl.BlockSpec((None,) + shape, lambda b, pt: (b,) + (0,) * len(shape))
    page = (SB_W, PAGE_SIZE)
    grid_spec = pltpu.PrefetchScalarGridSpec(
        num_scalar_prefetch=1,
        grid=(nb,),
        in_specs=[per_b((ROWS_S, SB_W)), per_b((t_new, SB_W)), per_b((t_new, SB_W)),
                  pl.BlockSpec(memory_space=pl.ANY), pl.BlockSpec(memory_space=pl.ANY)],
        out_specs=per_b((ROWS_S, HEAD_DIM)),
        scratch_shapes=[pltpu.VMEM((ROWS_S, PAGE_SIZE), F32), pltpu.VMEM((ROWS_S, SB_W), F32),
                        pltpu.VMEM((PAGE_SIZE, SB_W), F32), pltpu.VMEM((PAGE_SIZE, SB_W), F32),
                        pltpu.SMEM((1,), F32),
                        pltpu.VMEM((2, SB_AHEAD) + page, F32), pltpu.VMEM((2, SB_AHEAD) + page, F32),
                        pltpu.SemaphoreType.DMA((2, SB_AHEAD)), pltpu.SemaphoreType.DMA((2, SB_AHEAD)),
                        pltpu.VMEM(page, F32), pltpu.VMEM(page, F32), pltpu.SemaphoreType.DMA((2,))])
    return pl.pallas_call(
        _sb_sample_kernel,
        grid_spec=grid_spec,
        out_shape=jax.ShapeDtypeStruct((nb, ROWS_S, HEAD_DIM), F32),
        compiler_params=_cparams(("arbitrary",)),
        name="sb_sample",
    )(page_table, qs, new_k, new_v, cache_k, cache_v)


def _sample_group(xs, f32, page_table, caches, states, rel_bias, cmp_w_k, cmp_w_v, past):
    cache_sb_k, cache_sb_v, cache_cmp_k, cache_cmp_v, cache_sel_k, cache_sel_v = caches
    win_k, win_v = states
    nb, n_pages = page_table.shape
    t_new = xs.shape[0] // nb
    rs = lambda a, w: a.reshape(nb, t_new, w)

    sq = f32["sq"].reshape(nb, t_new, SB_HEADS, HEAD_DIM).transpose(0, 2, 1, 3)
    qs = jnp.einsum("bhtd,hk->bhtkd", sq, jnp.eye(SB_HEADS, dtype=F32))
    qs = qs.reshape(nb, ROWS_S, SB_W).astype(BF16)
    nq = f32["nq"].reshape(nb, t_new, NSA_KV_HEADS, NSA_GROUP, HEAD_DIM).transpose(0, 3, 2, 1, 4)
    qn = jnp.einsum("bgktd,kj->bgktjd", nq, jnp.eye(NSA_KV_HEADS, dtype=F32))
    qn = qn.reshape(nb, ROWS_S, KV_W).astype(BF16)

    o_sb = _sb_sample(page_table, _paged(cache_sb_k), _paged(cache_sb_v), qs,
                      rs(f32["sk"], SB_W), rs(f32["sv"], SB_W))
    o_sb = o_sb.reshape(nb, SB_HEADS, t_new, HEAD_DIM).transpose(0, 2, 1, 3).reshape(nb * t_new, SB_W)

    n_cmp = past // CMP_STRIDE
    n_sel = -(-(past + t_new) // SEL_BLOCK)
    n_sel_pad = -(-2 * (n_pages + 1) // LANES) * LANES
    w_buf = win_k.shape[1]
    w_pad = w_buf + LANES
    nn = np.arange(n_cmp)[None, :]
    jj = np.arange(n_sel_pad)[:, None]
    a_t = jnp.asarray(((nn >= SEL_PER_CMP * jj - 1) & (nn <= SEL_PER_CMP * jj + SEL_PER_CMP - 1)
                       & (jj < n_sel)).astype(np.float32), BF16)
    r = np.arange(ROWS_S)
    head_of_row = ((r % KT) // 4) * NSA_GROUP + r // KT
    rbr = rel_bias.T[head_of_row]
    bsc, bsw, bss = _bias_sample(rbr, past, w_buf, n_cmp, w_pad)

    o_c, o_w, sel = _nsa_sample_cmp(
        page_table, _paged(cache_cmp_k), _paged(cache_cmp_v), qn, rs(f32["ck"], KV_W), rs(f32["cv"], KV_W),
        _paged(win_k), _paged(win_v), rs(f32["wk"], KV_W), rs(f32["wv"], KV_W),
        cmp_w_k, cmp_w_v, a_t, bsc, bsw, past=past)

    mask_pages = sel[:, :2 * (n_pages + 1)].reshape(nb, n_pages + 1, 2, KT).transpose(0, 1, 3, 2)
    mask_pages = jnp.tile(mask_pages, (1, 1, NSA_GROUP, 1))
    gates = f32["gt"][:, :GATE_W].reshape(nb, t_new, 3, NSA_KV_HEADS, NSA_GROUP)
    gates = gates.transpose(0, 4, 3, 1, 2).reshape(nb, ROWS_S, 3)
    o_nsa = _nsa_sample_sel(page_table, _paged(cache_sel_k), _paged(cache_sel_v), qn, mask_pages,
                            rs(f32["sk2"], KV_W), rs(f32["sv2"], KV_W), bss, o_c, o_w, gates, past=past)
    o_nsa = o_nsa.reshape(nb, NSA_GROUP, NSA_KV_HEADS, t_new, HEAD_DIM).transpose(0, 3, 2, 1, 4)
    return o_sb, o_nsa.reshape(nb * t_new, NSA_W)


def kernel(x_prompt, x_sample, cache_sb_k, cache_sb_v, cache_cmp_k, cache_cmp_v, cache_sel_k, cache_sel_v, state_win_k, state_win_v, page_table, ffn1_pre_g, ffn1_post_g, ffn1_w_gate, ffn1_w_up, ffn1_w_down, mix_pre_g, mix_post_g, w_in, cmp_pos_k, cmp_pos_v, w_cmp_k1, w_cmp_k2, w_cmp_v1, w_cmp_v2, sb_out_g, nsa_out_g, w_o, rel_bias, ffn2_pre_g, ffn2_post_g, ffn2_w_gate, ffn2_w_up, ffn2_w_down):
    depth = w_in.shape[0]
    assert depth == 1, "caches are updated for a single layer"
    nb_p, s_len, d_model = x_prompt.shape
    nb_s, t_new, _ = x_sample.shape
    assert nb_p == 1 and t_new == 4 and s_len % (CMP_CHUNK_SPAN) == 0
    n_pages = page_table.shape[1]
    past = n_pages * PAGE_SIZE
    assert n_pages % PAGES_PER_STEP == 0 and past % SEL_BLOCK == 0
    w_keep = min(WINDOW, s_len)
    tm_p = 512
    tm_s = min(512, nb_s * t_new)
    l = 0

    ffn1 = (ffn1_pre_g[l], ffn1_post_g[l], ffn1_w_gate[l].astype(BF16), ffn1_w_up[l].astype(BF16),
            ffn1_w_down[l].astype(BF16))
    ffn2 = (ffn2_pre_g[l], ffn2_post_g[l], ffn2_w_gate[l].astype(BF16), ffn2_w_up[l].astype(BF16),
            ffn2_w_down[l].astype(BF16))
    w_in_pad = _pad_w_in(w_in[l])
    w_o_b = w_o[l].astype(BF16)
    cmp_w_k = _compress_weights(w_cmp_k1[l], w_cmp_k2[l], cmp_pos_k[l])
    cmp_w_v = _compress_weights(w_cmp_v1[l], w_cmp_v2[l], cmp_pos_v[l])

    xp = _ffn(x_prompt[0], *ffn1, tm=tm_p)
    pf, pb = _in_proj(xp, mix_pre_g[l], w_in_pad, tm=tm_p)
    o_sb = _sb_prompt(pb["sq"], pb["sk"], pb["sv"])
    o_nsa = _nsa_prompt_group(rel_bias, pf, pb, cmp_w_k, cmp_w_v)
    xp = _out_proj(xp, o_sb, o_nsa, sb_out_g[l], nsa_out_g[l], w_o_b, mix_post_g[l], tm=tm_p)
    xp = _ffn(xp, *ffn2, tm=tm_p)

    xs = _ffn(x_sample.reshape(nb_s * t_new, d_model), *ffn1, tm=tm_s)
    sf, _ = _in_proj(xs, mix_pre_g[l], w_in_pad, tm=tm_s)
    caches = (cache_sb_k[l], cache_sb_v[l], cache_cmp_k[l], cache_cmp_v[l], cache_sel_k[l], cache_sel_v[l])
    win = (state_win_k[l], state_win_v[l])
    o_sb_s, o_nsa_s = _sample_group(xs, sf, page_table, caches, win, rel_bias, cmp_w_k, cmp_w_v, past)
    xs = _out_proj(xs, o_sb_s, o_nsa_s, sb_out_g[l], nsa_out_g[l], w_o_b, mix_post_g[l], tm=tm_s)
    xs = _ffn(xs, *ffn2, tm=tm_s)

    sbh = lambda a, n: a.reshape(1, -1, n, SB_HEADS, HEAD_DIM)
    kvh = lambda a, n: a.reshape(1, -1, n, NSA_KV_HEADS, HEAD_DIM)
    prompt_kv = [kvh(pf[name], s_len) for name in ("ck", "cv", "sk2", "sv2")]
    sample_kv = [kvh(sf[name], t_new) for name in ("ck", "cv", "sk2", "sv2")]
    new_wk = jnp.concatenate([state_win_k[l].reshape(nb_s, -1, KV_W), sf["wk"].reshape(nb_s, t_new, KV_W)],
                             axis=1)[:, t_new:]
    new_wv = jnp.concatenate([state_win_v[l].reshape(nb_s, -1, KV_W), sf["wv"].reshape(nb_s, t_new, KV_W)],
                             axis=1)[:, t_new:]
    return (xp[None], xs.reshape(nb_s, t_new, d_model),
            sbh(pf["sk"], s_len), sbh(pf["sv"], s_len), *prompt_kv,
            kvh(pf["wk"][s_len - w_keep:], w_keep), kvh(pf["wv"][s_len - w_keep:], w_keep),
            sbh(sf["sk"], t_new), sbh(sf["sv"], t_new), *sample_kv,
            kvh(new_wk, new_wk.shape[1]), kvh(new_wv, new_wv.shape[1]))
```

```python
import functools
import math

import numpy as np
import jax
import jax.numpy as jnp
from jax import lax
from jax.experimental import pallas as pl
from jax.experimental.pallas import tpu as pltpu

F32 = jnp.float32
BF16 = jnp.bfloat16

HEAD_DIM = 64
SB_HEADS = 8
NSA_HEADS = 8
NSA_KV_HEADS = 2
NSA_GROUP = 4
CMP_BLOCK = 32
CMP_STRIDE = 16
SEL_BLOCK = 64
SEL_TOPK = 16
SEL_PER_CMP = SEL_BLOCK // CMP_STRIDE
WINDOW = 512
Q_BLOCK = 128
NUM_BUCKETS = 32
MAX_EXACT = 16
MAX_DISTANCE = 1024
RMS_EPS = 1e-6
PAGE_SIZE = 128
SCALE = HEAD_DIM ** -0.5
LOG2E = math.log2(math.e)

LANES = 128
VMEM_LIMIT = 56 * 1024 * 1024
NEG = -1e30
EXP_ZERO = -104.0


def _bucket_thresholds():
    n = np.arange(0, 4 * MAX_DISTANCE)
    nf = np.maximum(n, 1).astype(np.float64)
    large = MAX_EXACT + (np.log(nf / MAX_EXACT) / math.log(MAX_DISTANCE / MAX_EXACT)
                         * (NUM_BUCKETS - MAX_EXACT)).astype(np.int64)
    bucket = np.where(n < MAX_EXACT, n, np.minimum(large, NUM_BUCKETS - 1))
    return [int(np.argmax(bucket >= b)) for b in range(1, NUM_BUCKETS)]


BUCKET_THR = _bucket_thresholds()
FAR_DIST = BUCKET_THR[-1]


def _cparams(sem):
    return pltpu.CompilerParams(dimension_semantics=sem, vmem_limit_bytes=VMEM_LIMIT)


def _dot(a, b):
    return jnp.dot(a, b, preferred_element_type=F32)


def _dot_nt(a, b):
    return lax.dot_general(a, b, (((1,), (1,)), ((), ())), preferred_element_type=F32)


def _dot_tn(a, b):
    return lax.dot_general(a, b, (((0,), (0,)), ((), ())), preferred_element_type=F32)


def _split_dot(x, w):
    hi = x.astype(BF16)
    lo = (x - hi.astype(F32)).astype(BF16)
    return _dot(hi, w) + _dot(lo, w)


def _split_dot_left(w, x):
    hi = x.astype(BF16)
    lo = (x - hi.astype(F32)).astype(BF16)
    return _dot(w, hi) + _dot(w, lo)


def _rms(x, g):
    return x * lax.rsqrt(jnp.mean(x * x, axis=-1, keepdims=True) + RMS_EPS) * g


def _ffn_kernel(x_ref, pre_ref, post_ref, wg_ref, wu_ref, wd_ref, o_ref, acc_ref, *, fc):
    x = x_ref[...]
    hb = _rms(x, pre_ref[...]).astype(BF16)
    n_f = wg_ref.shape[1]
    for c in range(n_f // fc):
        g = _dot(hb, wg_ref[:, c * fc:(c + 1) * fc])
        u = _dot(hb, wu_ref[:, c * fc:(c + 1) * fc])
        a = (jax.nn.silu(g) * u).astype(BF16)
        part = _dot(a, wd_ref[c * fc:(c + 1) * fc, :])
        if c == 0:
            acc_ref[...] = part
        else:
            acc_ref[...] += part
    o_ref[...] = x + 0.5 * _rms(acc_ref[...], post_ref[...])


def _ffn(x, pre_g, post_g, wg, wu, wd, *, tm):
    t, d = x.shape
    f = wg.shape[1]
    fc = 256
    const = lambda i: (0, 0)
    return pl.pallas_call(
        functools.partial(_ffn_kernel, fc=fc),
        grid=(t // tm,),
        in_specs=[
            pl.BlockSpec((tm, d), lambda i: (i, 0)),
            pl.BlockSpec((1, d), const),
            pl.BlockSpec((1, d), const),
            pl.BlockSpec((d, f), const, pipeline_mode=pl.Buffered(1)),
            pl.BlockSpec((d, f), const, pipeline_mode=pl.Buffered(1)),
            pl.BlockSpec((f, d), const, pipeline_mode=pl.Buffered(1)),
        ],
        out_specs=pl.BlockSpec((tm, d), lambda i: (i, 0)),
        out_shape=jax.ShapeDtypeStruct((t, d), F32),
        scratch_shapes=[pltpu.VMEM((tm, d), F32)],
        compiler_params=_cparams(("parallel",)),
        name="ffn_half",
    )(x, pre_g.reshape(1, d), post_g.reshape(1, d), wg, wu, wd)


SB_W = SB_HEADS * HEAD_DIM
NSA_W = NSA_HEADS * HEAD_DIM
KV_W = NSA_KV_HEADS * HEAD_DIM
GATE_W = 3 * NSA_HEADS
IN_PIECES = (("sq", SB_W), ("sk", SB_W), ("sv", SB_W), ("nq", NSA_W),
             ("ck", KV_W), ("cv", KV_W), ("sk2", KV_W), ("sv2", KV_W),
             ("wk", KV_W), ("wv", KV_W), ("gt", LANES))
IN_BF16 = ("sq", "sk", "sv", "nq", "sk2", "sv2", "wk", "wv")


def _in_proj_kernel(x_ref, g_ref, w_ref, *out_refs):
    hb = _rms(x_ref[...], g_ref[...]).astype(BF16)
    f32_refs = out_refs[:len(IN_PIECES)]
    bf_refs = dict(zip(IN_BF16, out_refs[len(IN_PIECES):]))
    off = 0
    for (name, width), o_ref in zip(IN_PIECES, f32_refs):
        p = _dot(hb, w_ref[:, off:off + width])
        o_ref[...] = p
        if name in bf_refs:
            bf_refs[name][...] = p.astype(BF16)
        off += width


def _in_proj(x, pre_g, w_in_pad, *, tm):
    t, d = x.shape
    wp = w_in_pad.shape[1]
    row = lambda i: (i, 0)
    const = lambda i: (0, 0)
    widths = dict(IN_PIECES)
    out_shape = ([jax.ShapeDtypeStruct((t, w), F32) for _, w in IN_PIECES]
                 + [jax.ShapeDtypeStruct((t, widths[n]), BF16) for n in IN_BF16])
    out_specs = ([pl.BlockSpec((tm, w), row) for _, w in IN_PIECES]
                 + [pl.BlockSpec((tm, widths[n]), row) for n in IN_BF16])
    outs = pl.pallas_call(
        _in_proj_kernel,
        grid=(t // tm,),
        in_specs=[pl.BlockSpec((tm, d), row),
                  pl.BlockSpec((1, d), const),
                  pl.BlockSpec((d, wp), const, pipeline_mode=pl.Buffered(1))],
        out_specs=out_specs,
        out_shape=out_shape,
        compiler_params=_cparams(("parallel",)),
        name="in_proj",
    )(x, pre_g.reshape(1, d), w_in_pad)
    f32 = {n: o for (n, _), o in zip(IN_PIECES, outs[:len(IN_PIECES)])}
    bf = dict(zip(IN_BF16, outs[len(IN_PIECES):]))
    return f32, bf


def _out_proj_kernel(x_ref, osb_ref, onsa_ref, sbg_ref, nsag_ref, wo_ref, post_ref, o_ref):
    a = _rms(osb_ref[...], sbg_ref[...]).astype(BF16)
    b = _rms(onsa_ref[...], nsag_ref[...]).astype(BF16)
    y = _dot(a, wo_ref[:SB_W, :]) + _dot(b, wo_ref[SB_W:, :])
    o_ref[...] = x_ref[...] + _rms(y, post_ref[...])


def _out_proj(x, o_sb, o_nsa, sb_g, nsa_g, w_o, post_g, *, tm):
    t, d = x.shape
    row = lambda i: (i, 0)
    const = lambda i: (0, 0)
    return pl.pallas_call(
        _out_proj_kernel,
        grid=(t // tm,),
        in_specs=[pl.BlockSpec((tm, d), row),
                  pl.BlockSpec((tm, SB_W), row),
                  pl.BlockSpec((tm, NSA_W), row),
                  pl.BlockSpec((1, SB_W), const),
                  pl.BlockSpec((1, NSA_W), const),
                  pl.BlockSpec((SB_W + NSA_W, d), const, pipeline_mode=pl.Buffered(1)),
                  pl.BlockSpec((1, d), const)],
        out_specs=pl.BlockSpec((tm, d), row),
        out_shape=jax.ShapeDtypeStruct((t, d), F32),
        compiler_params=_cparams(("parallel",)),
        name="out_proj",
    )(x, o_sb, o_nsa, sb_g.reshape(1, SB_W), nsa_g.reshape(1, NSA_W), w_o, post_g.reshape(1, d))


def _sb_tri():
    r = lax.broadcasted_iota(jnp.int32, (Q_BLOCK, 2 * Q_BLOCK), 0)
    c = lax.broadcasted_iota(jnp.int32, (Q_BLOCK, 2 * Q_BLOCK), 1)
    return jnp.where((c >= Q_BLOCK) | (r > c), 1.0, 0.0).astype(BF16)


SB_TILE = 2 * Q_BLOCK
SB_HEADS_PER_STEP = 4
SB_LANES = SB_HEADS_PER_STEP * HEAD_DIM


def _log_sigmoid_neg(z):
    return -(jnp.maximum(z, 0.0) + jnp.log(1.0 + jnp.exp(-jnp.abs(z))))


def _sb_prompt_kernel(q_ref, k_ref, v_ref, o_ref, ls_ref, acc_ref):
    i = pl.program_id(1)
    r = lax.broadcasted_iota(jnp.int32, (SB_TILE, SB_TILE + Q_BLOCK), 0)
    c = lax.broadcasted_iota(jnp.int32, (SB_TILE, SB_TILE + Q_BLOCK), 1)
    tri = jnp.where((c >= SB_TILE) | (r > c), 1.0, 0.0).astype(BF16)
    q_pos = Q_BLOCK * i + lax.broadcasted_iota(jnp.int32, (Q_BLOCK, SB_TILE), 0)
    lane = lax.broadcasted_iota(jnp.int32, (Q_BLOCK, SB_TILE), 1)
    ls_ref[...] = jnp.zeros_like(ls_ref)
    acc_ref[...] = jnp.zeros_like(acc_ref)

    def tile(tt):
        hi = (i - 2 * tt + 1) * Q_BLOCK
        start = pl.multiple_of(jnp.maximum(hi - SB_TILE, 0), Q_BLOCK)
        k_pos = start + lane
        valid = (k_pos < q_pos) & (k_pos < hi)
        top = jnp.float32(NEG)
        for hh in range(SB_HEADS_PER_STEP):
            lanes = slice(hh * HEAD_DIM, (hh + 1) * HEAD_DIM)
            z = _dot_nt(q_ref[:, lanes], k_ref[pl.ds(start, SB_TILE), lanes]) * SCALE
            ln = _log_sigmoid_neg(z)
            st = _split_dot(jnp.where(valid, ln, 0.0), tri)
            ls = ls_ref[hh]
            a = jnp.exp(z + ln + st[:, :SB_TILE] + jnp.concatenate([ls, ls], axis=1))
            a = jnp.where(valid, a, 0.0)
            acc_ref[hh] += _dot(a.astype(BF16), v_ref[pl.ds(start, SB_TILE), lanes])
            ls = ls + st[:, SB_TILE:]
            ls_ref[hh] = ls
            top = jnp.maximum(top, jnp.max(ls))
        return top

    def cond(carry):
        tt, top = carry
        return (i - 2 * tt + 1 > 0) & (top > EXP_ZERO)

    def body(carry):
        tt, _ = carry
        return tt + 1, tile(tt)

    lax.while_loop(cond, body, (1, tile(0)))
    for hh in range(SB_HEADS_PER_STEP):
        o_ref[:, hh * HEAD_DIM:(hh + 1) * HEAD_DIM] = acc_ref[hh]


def _sb_prompt(q, k, v):
    t = q.shape[0]
    assert t >= SB_TILE
    return pl.pallas_call(
        _sb_prompt_kernel,
        grid=(SB_W // SB_LANES, t // Q_BLOCK),
        in_specs=[pl.BlockSpec((Q_BLOCK, SB_LANES), lambda h, i: (i, h)),
                  pl.BlockSpec((t, SB_LANES), lambda h, i: (0, h)),
                  pl.BlockSpec((t, SB_LANES), lambda h, i: (0, h))],
        out_specs=pl.BlockSpec((Q_BLOCK, SB_LANES), lambda h, i: (i, h)),
        out_shape=jax.ShapeDtypeStruct((t, SB_W), F32),
        scratch_shapes=[pltpu.VMEM((SB_HEADS_PER_STEP, Q_BLOCK, Q_BLOCK), F32),
                        pltpu.VMEM((SB_HEADS_PER_STEP, Q_BLOCK, HEAD_DIM), F32)],
        compiler_params=_cparams(("parallel", "parallel")),
        name="sb_prompt",
    )(q, k, v)


def _pad_w_in(w_in):
    wp = sum(w for _, w in IN_PIECES)
    return jnp.pad(w_in, ((0, 0), (0, wp - w_in.shape[1]))).astype(BF16)


def _bias_from_dist(dist, rb_at):
    out = jnp.where(dist >= BUCKET_THR[0], rb_at(1), rb_at(0))
    for b in range(2, NUM_BUCKETS):
        out = jnp.where(dist >= BUCKET_THR[b - 1], rb_at(b), out)
    return out


CMP_CHUNK = 128
CMP_CHUNK_SPAN = CMP_CHUNK * CMP_STRIDE
QB_PER_CHUNK = CMP_CHUNK_SPAN // Q_BLOCK
NEAR_TILES = -(-(FAR_DIST + Q_BLOCK - 1) // Q_BLOCK)
FAR_TILE = NEAR_TILES
WIN_TAIL_TILE = NEAR_TILES + 1
DEAD_TILE = NEAR_TILES + 2
BIAS_TILES = NEAR_TILES + 3
WIN_BLOCKS = WINDOW // Q_BLOCK


def _bias_prompt_kernel(rb_ref, bt_ref, bct_ref):
    h = pl.program_id(0)
    row = lax.broadcasted_iota(jnp.int32, (Q_BLOCK, Q_BLOCK), 0)
    col = lax.broadcasted_iota(jnp.int32, (Q_BLOCK, Q_BLOCK), 1)
    rb_at = lambda b: rb_ref[b, h]
    far = rb_ref[NUM_BUCKETS - 1, h]

    def tile(d, carry):
        bias = (_bias_from_dist(Q_BLOCK * d + col - row, rb_at) - far) * LOG2E
        bt_ref[d] = jnp.where((d > 0) | (row <= col), bias, NEG)
        return carry

    lax.fori_loop(0, FAR_TILE + 1, tile, 0)
    tail = (_bias_from_dist(Q_BLOCK * WIN_BLOCKS + col - row, rb_at) - far) * LOG2E
    bt_ref[WIN_TAIL_TILE] = jnp.where(row > col, tail, NEG)
    bt_ref[DEAD_TILE] = jnp.full((Q_BLOCK, Q_BLOCK), NEG, F32)

    def ctile(rw, carry):
        r = rw // 2
        w = rw % 2
        dist = (Q_BLOCK * r + col - CMP_STRIDE * row - (CMP_BLOCK - 1)
                + CMP_CHUNK_SPAN * (1 - w))
        bct_ref[r, w] = _bias_from_dist(dist, rb_at)
        return carry

    lax.fori_loop(0, 2 * QB_PER_CHUNK, ctile, 0)


def _bias_prompt(rel_bias):
    nh = rel_bias.shape[1]
    return pl.pallas_call(
        _bias_prompt_kernel,
        grid=(nh,),
        in_specs=[pl.BlockSpec(memory_space=pltpu.SMEM)],
        out_specs=[pl.BlockSpec((None, BIAS_TILES, Q_BLOCK, Q_BLOCK), lambda h: (h, 0, 0, 0)),
                   pl.BlockSpec((None, QB_PER_CHUNK, 2, CMP_CHUNK, Q_BLOCK), lambda h: (h, 0, 0, 0, 0))],
        out_shape=[jax.ShapeDtypeStruct((nh, BIAS_TILES, Q_BLOCK, Q_BLOCK), F32),
                   jax.ShapeDtypeStruct((nh, QB_PER_CHUNK, 2, CMP_CHUNK, Q_BLOCK), F32)],
        compiler_params=_cparams(("parallel",)),
        name="bias_prompt",
    )(rel_bias)


GROUP_W = CMP_STRIDE * KV_W


def _compress_weights(w1, w2, pos):
    eye = jnp.eye(NSA_KV_HEADS, dtype=w1.dtype)
    halves = []
    for half in range(2):
        wh = w1[half * CMP_STRIDE:(half + 1) * CMP_STRIDE]
        halves.append(jnp.einsum("lde,pk->lpdke", wh, eye).reshape(GROUP_W, KV_W))
    w_big = jnp.concatenate(halves, axis=1).astype(BF16)
    w_pos = jnp.tile(w1.reshape(CMP_BLOCK * HEAD_DIM, HEAD_DIM), (1, NSA_KV_HEADS)).astype(BF16)
    pos_row = jnp.pad(pos.reshape(1, CMP_BLOCK * HEAD_DIM), ((0, 7), (0, 0))).astype(BF16)
    w2_bd = jnp.einsum("ef,pk->pekf", w2, eye).reshape(KV_W, KV_W).astype(BF16)
    return w_big, _pos_term(pos_row, w_pos), w2_bd


def _pos_term_kernel(pos_ref, w_ref, o_ref):
    o_ref[...] = _dot(pos_ref[...], w_ref[...])


def _pos_term(pos_row, w_pos):
    return pl.pallas_call(
        _pos_term_kernel,
        out_shape=jax.ShapeDtypeStruct((pos_row.shape[0], KV_W), F32),
        name="pos_term",
    )(pos_row, w_pos)


def _compress_finish(ab_ref, n_rows, c_rows, w2_bd):
    h = ab_ref[0:n_rows, 0:KV_W] + ab_ref[1:n_rows + 1, KV_W:2 * KV_W] + c_rows[0:1, :]
    return _dot(jax.nn.silu(h).astype(BF16), w2_bd)


def _compress_kernel(g_ref, wbig_ref, c_ref, w2_ref, o_ref, ab_ref):
    n = g_ref.shape[0]
    ab_ref[0:n, :] = _dot(g_ref[...].astype(BF16), wbig_ref[...])
    ab_ref[n:n + 8, :] = jnp.zeros((8, 2 * KV_W), F32)
    o_ref[...] = _compress_finish(ab_ref, n, c_ref[...], w2_ref[...])


def _compress_prompt(rows, w_big, c_rows, w2_bd):
    g = rows.reshape(rows.shape[0] // CMP_STRIDE, GROUP_W)
    n = g.shape[0]
    return pl.pallas_call(
        _compress_kernel,
        out_shape=jax.ShapeDtypeStruct((n, KV_W), F32),
        scratch_shapes=[pltpu.VMEM((n + 8, 2 * KV_W), F32)],
        compiler_params=pltpu.CompilerParams(vmem_limit_bytes=VMEM_LIMIT),
        name="compress_prompt",
    )(g, w_big, c_rows, w2_bd)


BIG = 3e38
FORCED_BLOCKS = 3
SEL_U = NEAR_TILES // 2
GQ = NSA_GROUP * Q_BLOCK


def _importance_matrices(n_half, n_cmp):
    r = np.arange(n_half)[:, None]
    n = np.arange(n_cmp)[None, :]
    mats = []
    for parity in range(2):
        j = 2 * r + parity
        mats.append(((n >= SEL_PER_CMP * j - 1) & (n <= SEL_PER_CMP * j + SEL_PER_CMP - 1)).astype(np.float32))
    return jnp.asarray(mats[0], BF16), jnp.asarray(mats[1], BF16)


def _topk_mask(imp_e, imp_o, cur, n_top):
    shape = imp_e.shape
    jr = lax.broadcasted_iota(jnp.int32, shape, 0)
    je = (2 * jr).astype(F32)
    jo = (2 * jr + 1).astype(F32)
    curf = cur.astype(F32)

    def encode(imp, j):
        valid = j <= curf
        forced = (j == 0.0) | (j == curf) | (j == curf - 1.0)
        return (jnp.where(valid, jnp.where(forced, -2.0, imp), -1.0), valid,
                jnp.where(forced & valid, 1.0, 0.0))

    se, valid_e, sel_e = encode(imp_e, je)
    so, valid_o, sel_o = encode(imp_o, jo)
    for _ in range(n_top - FORCED_BLOCKS):
        mx = jnp.maximum(jnp.max(se, axis=0, keepdims=True), jnp.max(so, axis=0, keepdims=True))
        cand = jnp.minimum(jnp.min(jnp.where(se == mx, je, BIG), axis=0, keepdims=True),
                           jnp.min(jnp.where(so == mx, jo, BIG), axis=0, keepdims=True))
        hit_e = je == cand
        hit_o = jo == cand
        sel_e = jnp.where(hit_e, 1.0, sel_e)
        sel_o = jnp.where(hit_o, 1.0, sel_o)
        se = jnp.where(hit_e, -2.0, se)
        so = jnp.where(hit_o, -2.0, so)
    return jnp.where(valid_e, sel_e, 0.0), jnp.where(valid_o, sel_o, 0.0)


def _nsa_prompt_kernel(rb_ref, q_ref, ck_ref, cvT_ref, sk_ref, svT_ref, wk_ref, wvT_ref, gtT_ref,
                       bt_ref, bct_ref, ae_ref, ao_ref, o_ref, sc_ref, sele_ref, selo_ref,
                       sa_ref, sb_ref, pa_ref, pb_ref, *, n_top):
    k = pl.program_id(0)
    i = pl.program_id(1)
    n_grp = ck_ref.shape[0]
    q = jnp.concatenate(
        [(q_ref[:, g * HEAD_DIM:(g + 1) * HEAD_DIM].astype(F32) * SCALE).astype(BF16)
         for g in range(NSA_GROUP)], axis=0)
    rb_far = [rb_ref[NUM_BUCKETS - 1, NSA_GROUP * k + g] for g in range(NSA_GROUP)]
    row = lax.broadcasted_iota(jnp.int32, (Q_BLOCK, Q_BLOCK), 0)
    col = lax.broadcasted_iota(jnp.int32, (Q_BLOCK, Q_BLOCK), 1)
    q_pos = Q_BLOCK * i + col
    gsl = lambda g: slice(g * Q_BLOCK, (g + 1) * Q_BLOCK)
    diag = i // QB_PER_CHUNK

    def cmp_scores(c, dst, near_w, live):
        cs = pl.multiple_of(c * CMP_CHUNK, CMP_CHUNK)
        s = _dot_nt(ck_ref[pl.ds(cs, CMP_CHUNK), :], q)
        if near_w is not None:
            valid = (CMP_STRIDE * (cs + row) + (CMP_BLOCK - 1) <= q_pos) & live
        parts = []
        for g in range(NSA_GROUP):
            if near_w is None:
                parts.append(s[:, gsl(g)] + rb_far[g])
            else:
                parts.append(jnp.where(valid, s[:, gsl(g)] + bct_ref[g, 0, near_w], NEG))
        s = jnp.concatenate(parts, axis=1)
        sc_ref[pl.ds(pl.multiple_of(dst, CMP_CHUNK), CMP_CHUNK), :] = s
        return s

    def cmp_step(ml, s):
        m, l = ml
        m_new = jnp.maximum(m, jnp.max(s, axis=0, keepdims=True))
        l = l * jnp.exp(m - m_new) + jnp.sum(jnp.exp(s - m_new), axis=0, keepdims=True)
        return m_new, l

    ml = lax.fori_loop(0, jnp.maximum(diag - 1, 0),
                       lambda c, ml: cmp_step(ml, cmp_scores(c, c * CMP_CHUNK, None, None)),
                       (jnp.full((1, GQ), NEG, F32), jnp.zeros((1, GQ), F32)))
    has_prev = diag >= 1
    ml = cmp_step(ml, cmp_scores(jnp.maximum(diag - 1, 0),
                                 jnp.where(has_prev, (diag - 1) * CMP_CHUNK, n_grp), 0, has_prev))
    m_c, l_c = cmp_step(ml, cmp_scores(diag, diag * CMP_CHUNK, 1, True))
    m_c = jnp.where(m_c > 0.5 * NEG, m_c, 0.0)
    r_c = 1.0 / jnp.maximum(l_c, 1e-30)

    n_half = sele_ref.shape[0]

    def cmp_out(c, carry):
        oc, ie, io = carry
        cs = pl.multiple_of(c * CMP_CHUNK, CMP_CHUNK)
        p = jnp.exp(sc_ref[pl.ds(cs, CMP_CHUNK), :] - m_c) * r_c
        oc = oc + _dot(cvT_ref[:, pl.ds(cs, CMP_CHUNK)], p.astype(BF16))
        pg = p[:, gsl(0)] + p[:, gsl(1)] + p[:, gsl(2)] + p[:, gsl(3)]
        ie = ie + _split_dot_left(ae_ref[:, pl.ds(cs, CMP_CHUNK)], pg)
        io = io + _split_dot_left(ao_ref[:, pl.ds(cs, CMP_CHUNK)], pg)
        return oc, ie, io

    o_c, imp_e, imp_o = lax.fori_loop(
        0, diag + 1, cmp_out,
        (jnp.zeros((HEAD_DIM, GQ), F32), jnp.zeros((n_half, Q_BLOCK), F32), jnp.zeros((n_half, Q_BLOCK), F32)))

    cur = (Q_BLOCK * i + lax.broadcasted_iota(jnp.int32, (n_half, Q_BLOCK), 1)) // SEL_BLOCK
    sel_e, sel_o = _topk_mask(imp_e, imp_o, cur, n_top)
    sele_ref[...] = sel_e
    selo_ref[...] = sel_o

    q2 = jnp.concatenate(
        [(q_ref[:, g * HEAD_DIM:(g + 1) * HEAD_DIM].astype(F32) * (SCALE * LOG2E)).astype(BF16)
         for g in range(NSA_GROUP)], axis=0)
    def scores(keys_ref, d):
        cs = pl.multiple_of(jnp.maximum(i - d, 0) * Q_BLOCK, Q_BLOCK)
        return _dot_nt(keys_ref[pl.ds(cs, Q_BLOCK), :], q2)

    def values(valsT_ref, d):
        cs = pl.multiple_of(jnp.maximum(i - d, 0) * Q_BLOCK, Q_BLOCK)
        return valsT_ref[:, pl.ds(cs, Q_BLOCK)]

    def biased(s, tile, extra):
        parts = []
        for g in range(NSA_GROUP):
            if tile is None:
                add = extra
            else:
                add = bt_ref[g, tile] if extra is None else bt_ref[g, tile] + extra
            parts.append(s[:, gsl(g)] + add)
        return jnp.concatenate(parts, axis=1)

    def update(s_list, m, acc):
        m_new = m
        for s in s_list:
            m_new = jnp.maximum(m_new, jnp.max(s, axis=0, keepdims=True))
        return [jnp.exp2(s - m_new).astype(BF16) for s in s_list], m_new, jnp.exp2(m - m_new) * acc

    def tile_of(d, last):
        return jnp.where(d <= i, jnp.minimum(d, last), DEAD_TILE)

    def sel_mask(d, guard):
        c2 = jnp.maximum(i - d, 0)

        def off(r):
            r = jnp.where(r > 0.0, 0.0, NEG)
            if guard:
                r = jnp.where(d <= i, r, NEG)
            return jnp.broadcast_to(r, (SEL_BLOCK, Q_BLOCK))

        return jnp.concatenate([off(sele_ref[pl.ds(c2, 1), :]), off(selo_ref[pl.ds(c2, 1), :])], axis=0)

    acc_rows = svT_ref.shape[0]
    m0 = jnp.full((1, GQ), NEG, F32)
    acc0 = jnp.zeros((acc_rows, GQ), F32)

    def sel_trip(d0, far, s_cur, p_cur, s_nxt, p_prv, m, acc):
        for u in range(SEL_U):
            s_nxt[u] = scores(sk_ref, d0 + SEL_U + u)
        pv = _dot(values(svT_ref, jnp.maximum(d0 - SEL_U, 0)), p_prv[0])
        for u in range(1, SEL_U):
            pv = pv + _dot(values(svT_ref, jnp.maximum(d0 - SEL_U + u, 0)), p_prv[u])
        tile = (lambda d: None) if far else (lambda d: tile_of(d, NEAR_TILES - 1))
        sb = [biased(s_cur[u], tile(d0 + u), sel_mask(d0 + u, far)) for u in range(SEL_U)]
        p, m, acc = update(sb, m, acc + pv)
        for u in range(SEL_U):
            p_cur[u] = p[u]
        return m, acc

    def sel_round(j, far, m, acc):
        m, acc = sel_trip(2 * SEL_U * j, far, sa_ref, pa_ref, sb_ref, pb_ref, m, acc)
        return sel_trip(2 * SEL_U * j + SEL_U, far, sb_ref, pb_ref, sa_ref, pa_ref, m, acc)

    for u in range(SEL_U):
        sa_ref[u] = scores(sk_ref, u)
    pb_ref[...] = jnp.zeros(pb_ref.shape, BF16)
    n_trips = i // SEL_U + 1
    n_rounds = (n_trips + 1) // 2
    m, acc = sel_round(0, False, m0, acc0)
    _, o_s = lax.fori_loop(1, n_rounds, lambda j, c: sel_round(j, True, *c), (m, acc))
    last = 2 * SEL_U * n_rounds - SEL_U
    for u in range(SEL_U):
        o_s = o_s + _dot(values(svT_ref, last + u), pb_ref[u])

    s_w = [scores(wk_ref, d) for d in range(WIN_BLOCKS + 1)]
    m, o_w = m0, acc0
    for d in range(WIN_BLOCKS + 1):
        tile = tile_of(d, NEAR_TILES - 1) if d < WIN_BLOCKS else jnp.where(d <= i, WIN_TAIL_TILE, DEAD_TILE)
        (p,), m, o_w = update([biased(s_w[d], tile, None)], m, o_w)
        o_w = o_w + _dot(values(wvT_ref, d), p)

    r_s = 1.0 / jnp.maximum(o_s[HEAD_DIM:HEAD_DIM + 1, :], 1e-30)
    r_w = 1.0 / jnp.maximum(o_w[HEAD_DIM:HEAD_DIM + 1, :], 1e-30)
    for g in range(NSA_GROUP):
        gate = lambda br: jax.nn.sigmoid(gtT_ref[pl.ds(br * NSA_HEADS + NSA_GROUP * k + g, 1), :])
        o = (gate(0) * o_c[:, gsl(g)]
             + gate(1) * r_s[:, gsl(g)] * o_s[:HEAD_DIM, gsl(g)]
             + gate(2) * r_w[:, gsl(g)] * o_w[:HEAD_DIM, gsl(g)])
        o_ref[:, g * HEAD_DIM:(g + 1) * HEAD_DIM] = o.T


def _nsa_prompt(rel_bias, nq, ck, cvT, sk, svT, wk, wvT, gtT, bt, bct):
    t = nq.shape[0]
    n_grp = ck.shape[1]
    n_half = t // (2 * SEL_BLOCK)
    n_top = min(SEL_TOPK, t // SEL_BLOCK)
    a_e, a_o = _importance_matrices(n_half, n_grp)
    per_k3 = lambda shape: pl.BlockSpec((None,) + shape, lambda k, i: (k, 0, 0))
    const2 = lambda shape: pl.BlockSpec(shape, lambda k, i: (0, 0))
    return pl.pallas_call(
        functools.partial(_nsa_prompt_kernel, n_top=n_top),
        grid=(NSA_KV_HEADS, t // Q_BLOCK),
        in_specs=[pl.BlockSpec(memory_space=pltpu.SMEM),
                  pl.BlockSpec((Q_BLOCK, NSA_GROUP * HEAD_DIM), lambda k, i: (i, k)),
                  per_k3((n_grp, HEAD_DIM)), per_k3((HEAD_DIM, n_grp)),
                  per_k3((t, HEAD_DIM)), per_k3((VAL_ROWS, t)),
                  per_k3((t, HEAD_DIM)), per_k3((VAL_ROWS, t)),
                  pl.BlockSpec((GATE_W, Q_BLOCK), lambda k, i: (0, i)),
                  pl.BlockSpec((NSA_GROUP, BIAS_TILES, Q_BLOCK, Q_BLOCK), lambda k, i: (k, 0, 0, 0)),
                  pl.BlockSpec((NSA_GROUP, 1, 2, CMP_CHUNK, Q_BLOCK),
                               lambda k, i: (k, i % QB_PER_CHUNK, 0, 0, 0)),
                  const2((n_half, n_grp)), const2((n_half, n_grp))],
        out_specs=pl.BlockSpec((Q_BLOCK, NSA_GROUP * HEAD_DIM), lambda k, i: (i, k)),
        out_shape=jax.ShapeDtypeStruct((t, NSA_W), F32),
        scratch_shapes=[pltpu.VMEM((n_grp + CMP_CHUNK, GQ), F32),
                        pltpu.VMEM((n_half, Q_BLOCK), F32),
                        pltpu.VMEM((n_half, Q_BLOCK), F32),
                        pltpu.VMEM((SEL_U, Q_BLOCK, GQ), F32),
                        pltpu.VMEM((SEL_U, Q_BLOCK, GQ), F32),
                        pltpu.VMEM((SEL_U, Q_BLOCK, GQ), BF16),
                        pltpu.VMEM((SEL_U, Q_BLOCK, GQ), BF16)],
        compiler_params=_cparams(("parallel", "arbitrary")),
        name="nsa_prompt",
    )(rel_bias, nq, ck, cvT, sk, svT, wk, wvT, gtT, bt, bct, a_e, a_o)


def _per_kv_head(x):
    t = x.shape[0]
    xk = x.reshape(t, NSA_KV_HEADS, HEAD_DIM).transpose(1, 0, 2)
    return xk, xk.transpose(0, 2, 1)


VAL_ROWS = HEAD_DIM + 8


def _with_ones_row(v_t):
    heads, _, t = v_t.shape
    extra = jnp.zeros((heads, VAL_ROWS - HEAD_DIM, t), v_t.dtype).at[:, 0, :].set(1)
    return jnp.concatenate([v_t, extra], axis=1)


def _nsa_prompt_group(rel_bias, f32, bf, cmp_w_k, cmp_w_v):
    ck = _compress_prompt(f32["ck"], *cmp_w_k)
    cv = _compress_prompt(f32["cv"], *cmp_w_v)
    ck_h, _ = _per_kv_head(ck.astype(BF16))
    _, cv_t = _per_kv_head(cv.astype(BF16))
    sk_h, _ = _per_kv_head(bf["sk2"])
    sv_t = _with_ones_row(_per_kv_head(bf["sv2"])[1])
    wk_h, _ = _per_kv_head(bf["wk"])
    wv_t = _with_ones_row(_per_kv_head(bf["wv"])[1])
    gt_t = f32["gt"][:, :GATE_W].T
    bt, bct = _bias_prompt(rel_bias)
    return _nsa_prompt(rel_bias, bf["nq"], ck_h, cv_t, sk_h, sv_t, wk_h, wv_t, gt_t, bt, bct)


PAGES_PER_STEP = 32
ROWS_S = NSA_HEADS * 4
KT = NSA_KV_HEADS * 4


def _paged(cache):
    n_pool, rows, heads, d = cache.shape
    return cache.transpose(0, 2, 3, 1).reshape(n_pool, heads * d, rows)


def _page_specs(n, width, page_of):
    return [pl.BlockSpec((None, width, PAGE_SIZE),
                         functools.partial(lambda b, p, pt, c: (page_of(b, p, c, pt), 0, 0), c=c))
            for c in range(n)]


def _bias_sample_kernel(rbr_ref, bsc_ref, bsw_ref, bss_ref, *, past, w_buf):
    t_of = lambda shape: lax.broadcasted_iota(jnp.int32, shape, 0) % 4
    lane = lambda shape: lax.broadcasted_iota(jnp.int32, shape, 1)
    rb_at = lambda b: rbr_ref[:, b:b + 1]
    shp = bsc_ref.shape
    bsc_ref[...] = _bias_from_dist(past + t_of(shp) - CMP_STRIDE * lane(shp) - (CMP_BLOCK - 1), rb_at)
    shp = bsw_ref.shape
    bsw_ref[...] = _bias_from_dist(w_buf + t_of(shp) - lane(shp), rb_at)
    shp = (ROWS_S, LANES)

    def chunk(c, carry):
        cs = pl.multiple_of(c * LANES, LANES)
        bss_ref[:, pl.ds(cs, LANES)] = _bias_from_dist(past + t_of(shp) - cs - lane(shp), rb_at)
        return carry

    lax.fori_loop(0, bss_ref.shape[1] // LANES, chunk, 0)


def _bias_sample(rbr, past, w_buf, n_cmp_pad, w_pad):
    return pl.pallas_call(
        functools.partial(_bias_sample_kernel, past=past, w_buf=w_buf),
        out_shape=[jax.ShapeDtypeStruct((ROWS_S, n_cmp_pad), F32),
                   jax.ShapeDtypeStruct((ROWS_S, w_pad), F32),
                   jax.ShapeDtypeStruct((ROWS_S, past + PAGE_SIZE), F32)],
        compiler_params=pltpu.CompilerParams(vmem_limit_bytes=VMEM_LIMIT),
        name="bias_sample",
    )(rbr)


def _masked_softmax_lanes(s, valid):
    s = jnp.where(valid, s, NEG)
    m = jnp.max(s, axis=-1, keepdims=True)
    m = jnp.where(m > 0.5 * NEG, m, 0.0)
    e = jnp.where(valid, jnp.exp(s - m), 0.0)
    return e / jnp.maximum(jnp.sum(e, axis=-1, keepdims=True), 1e-30)


def _own_kv_lanes(o, kv_of_row):
    return jnp.where(kv_of_row == 0, o[:, :HEAD_DIM], o[:, HEAD_DIM:])


def _topk_mask_rows(imp, cur, n_blocks, n_top):
    j = lax.broadcasted_iota(jnp.int32, imp.shape, 0).astype(F32)
    curf = cur.astype(F32)
    valid = (j <= curf) & (j < float(n_blocks))
    forced = (j == 0.0) | (j == curf) | (j == curf - 1.0)
    score = jnp.where(valid, jnp.where(forced, -2.0, imp), -1.0)
    sel = jnp.where(forced & valid, 1.0, 0.0)
    for _ in range(n_top - FORCED_BLOCKS):
        mx = jnp.max(score, axis=0, keepdims=True)
        cand = jnp.min(jnp.where(score == mx, j, BIG), axis=0, keepdims=True)
        hit = j == cand
        sel = jnp.where(hit, 1.0, sel)
        score = jnp.where(hit, -2.0, score)
    return jnp.where(valid, sel, 0.0)


def _nsa_sample_cmp_kernel(pt_ref, *refs, past, n_top, n_sel):
    del pt_ref
    n = PAGES_PER_STEP
    kp = refs[:n]
    vp = refs[n:2 * n]
    (q_ref, nk_ref, nv_ref, wsk_ref, wsv_ref, nwk_ref, nwv_ref,
     wbk_ref, ck_ref, w2k_ref, wbv_ref, cv_ref, w2v_ref,
     at_ref, bsc_ref, bsw_ref,
     oc_ref, ow_ref, sel_ref,
     abk_ref, abv_ref, new8_ref, xt_ref, nwk128_ref, nwv128_ref) = refs[2 * n:]
    p = pl.program_id(1)
    n_steps = pl.num_programs(1)
    groups_per_page = PAGE_SIZE // CMP_STRIDE
    rows_per_step = n * groups_per_page
    n_cmp = abk_ref.shape[0] - 8
    w_buf = wsk_ref.shape[1]

    base = pl.multiple_of(p * rows_per_step, rows_per_step)
    pr = lax.broadcasted_iota(jnp.int32, (PAGE_SIZE, PAGE_SIZE), 0)
    pc = lax.broadcasted_iota(jnp.int32, (PAGE_SIZE, PAGE_SIZE), 1)
    perm = (pc == CMP_STRIDE * (pr % groups_per_page) + pr // groups_per_page).astype(BF16)
    for pages, wb_ref, ab_ref in ((kp, wbk_ref, abk_ref), (vp, wbv_ref, abv_ref)):
        for c in range(0, n, 2):
            two = jnp.concatenate([pages[c][...], pages[c + 1][...]], axis=0).astype(BF16)
            xt = _dot_nt(perm, two)
            xt_ref[c] = xt[:, :KV_W]
            xt_ref[c + 1] = xt[:, KV_W:]
        ab = jnp.zeros((rows_per_step, 2 * KV_W), F32)
        for l in range(0, CMP_STRIDE, 2):
            x_l = xt_ref[:, l * groups_per_page:(l + 2) * groups_per_page, :]
            x_l = jnp.concatenate([x_l[:, :groups_per_page].reshape(rows_per_step, KV_W),
                                   x_l[:, groups_per_page:].reshape(rows_per_step, KV_W)], axis=1)
            ab = ab + _dot(x_l.astype(BF16), wb_ref[l * KV_W:(l + 2) * KV_W, :])
        ab_ref[pl.ds(base, rows_per_step), :] = ab

    @pl.when(p == n_steps - 1)
    def _():
        r8 = lax.broadcasted_iota(jnp.int32, (8, KV_W), 0)
        kv_of_row = (lax.broadcasted_iota(jnp.int32, (ROWS_S, 1), 0) % KT) // 4
        t_row = lax.broadcasted_iota(jnp.int32, (ROWS_S, 1), 0) % 4
        q = q_ref[...]

        comp = []
        for new_ref, wb_ref, ab_ref, c_ref, w2_ref in (
                (nk_ref, wbk_ref, abk_ref, ck_ref, w2k_ref),
                (nv_ref, wbv_ref, abv_ref, cv_ref, w2v_ref)):
            new8_ref[...] = jnp.zeros((8, KV_W), F32)
            new8_ref[0:4, :] = new_ref[...]
            x = new8_ref[...]
            tot = jnp.zeros((8, 2 * KV_W), F32)
            for l in range(4):
                xl = jnp.where(r8 == l, x, 0.0).astype(BF16)
                tot = tot + _dot(xl, wb_ref[l * KV_W:(l + 1) * KV_W, :])
            ab_ref[n_cmp:n_cmp + 8, :] = jnp.where(
                lax.broadcasted_iota(jnp.int32, (8, 2 * KV_W), 0) == 0,
                jnp.sum(tot, axis=0, keepdims=True), 0.0)
            comp.append(_compress_finish(ab_ref, n_cmp, c_ref[...], w2_ref[...]))
        ck, cv = comp

        lane_c = lax.broadcasted_iota(jnp.int32, (ROWS_S, n_cmp), 1)
        valid_c = past + t_row - CMP_STRIDE * lane_c - (CMP_BLOCK - 1) >= 0
        s_c = _dot_nt(q, ck.astype(BF16)) * SCALE + bsc_ref[...]
        p_c = _masked_softmax_lanes(s_c, valid_c)
        oc_ref[...] = _own_kv_lanes(_dot(p_c.astype(BF16), cv.astype(BF16)), kv_of_row)

        pg = p_c[0:KT] + p_c[KT:2 * KT] + p_c[2 * KT:3 * KT] + p_c[3 * KT:4 * KT]
        pg = jnp.concatenate([pg, jnp.zeros((LANES - KT, n_cmp), F32)], axis=0)
        pg_hi = pg.astype(BF16)
        pg_lo = (pg - pg_hi.astype(F32)).astype(BF16)
        imp = _dot_nt(at_ref[...], pg_hi) + _dot_nt(at_ref[...], pg_lo)
        cur = (past + lax.broadcasted_iota(jnp.int32, (1, LANES), 1) % 4) // SEL_BLOCK
        sel_ref[...] = _topk_mask_rows(imp, cur, n_sel, n_top)[:, :KT]

        for nw_ref, pad_ref in ((nwk_ref, nwk128_ref), (nwv_ref, nwv128_ref)):
            pad_ref[...] = jnp.zeros(pad_ref.shape, F32)
            pad_ref[0:4, :] = nw_ref[...]
        lane_w = lax.broadcasted_iota(jnp.int32, (ROWS_S, w_buf + LANES), 1)
        dist_w = w_buf + t_row - lane_w
        valid_w = (dist_w >= 0) & (dist_w < WINDOW) & (past - w_buf + lane_w >= 0)
        s_w = jnp.concatenate([_dot(q, wsk_ref[...].astype(BF16)),
                               _dot_nt(q, nwk128_ref[...].astype(BF16))], axis=1) * SCALE + bsw_ref[...]
        p_w = _masked_softmax_lanes(s_w, valid_w).astype(BF16)
        o_w = _dot_nt(p_w[:, :w_buf], wsv_ref[...].astype(BF16)) + _dot(p_w[:, w_buf:], nwv128_ref[...].astype(BF16))
        ow_ref[...] = _own_kv_lanes(o_w, kv_of_row)


def _nsa_sample_cmp(page_table, cache_k, cache_v, qn, new_k, new_v, win_k, win_v, new_wk, new_wv,
                    cmp_w_k, cmp_w_v, a_t, bsc, bsw, *, past):
    nb, n_pages = page_table.shape
    n = PAGES_PER_STEP
    n_steps = n_pages // n
    t_new = new_k.shape[1]
    assert past % CMP_STRIDE == 0 and t_new <= CMP_STRIDE
    n_cmp = past // CMP_STRIDE
    n_sel = -(-(past + t_new) // SEL_BLOCK)
    n_top = min(SEL_TOPK, n_sel)
    w_buf = win_k.shape[2]
    page_of = lambda b, p, c, pt: pt[b, p * n + c]
    per_b = lambda shape: pl.BlockSpec((None,) + shape, lambda b, p, pt: (b,) + (0,) * len(shape))
    const = lambda a: pl.BlockSpec(a.shape, lambda b, p, pt: (0,) * a.ndim)
    weights = list(cmp_w_k) + list(cmp_w_v)
    out_shape = [jax.ShapeDtypeStruct((nb, ROWS_S, HEAD_DIM), F32),
                 jax.ShapeDtypeStruct((nb, ROWS_S, HEAD_DIM), F32),
                 jax.ShapeDtypeStruct((nb, a_t.shape[0], KT), F32)]
    grid_spec = pltpu.PrefetchScalarGridSpec(
        num_scalar_prefetch=1,
        grid=(nb, n_steps),
        in_specs=(_page_specs(n, KV_W, page_of) + _page_specs(n, KV_W, page_of)
                  + [per_b((ROWS_S, KV_W)), per_b((t_new, KV_W)), per_b((t_new, KV_W)),
                     per_b((KV_W, w_buf)), per_b((KV_W, w_buf)), per_b((t_new, KV_W)), per_b((t_new, KV_W))]
                  + [const(w) for w in weights] + [const(a_t), const(bsc), const(bsw)]),
        out_specs=[per_b(s.shape[1:]) for s in out_shape],
        scratch_shapes=[pltpu.VMEM((n_cmp + 8, 2 * KV_W), F32),
                        pltpu.VMEM((n_cmp + 8, 2 * KV_W), F32),
                        pltpu.VMEM((8, KV_W), F32),
                        pltpu.VMEM((n, PAGE_SIZE, KV_W), F32),
                        pltpu.VMEM((LANES, KV_W), F32),
                        pltpu.VMEM((LANES, KV_W), F32)])
    return pl.pallas_call(
        functools.partial(_nsa_sample_cmp_kernel, past=past, n_top=n_top, n_sel=n_sel),
        grid_spec=grid_spec,
        out_shape=out_shape,
        compiler_params=_cparams(("parallel", "arbitrary")),
        name="nsa_sample_cmp",
    )(page_table, *([cache_k] * n), *([cache_v] * n), qn, new_k, new_v, win_k, win_v, new_wk, new_wv,
      *weights, a_t, bsc, bsw)


def _nsa_sample_sel_kernel(pt_ref, *refs, past):
    del pt_ref
    n = PAGES_PER_STEP
    kp = refs[:n]
    vp = refs[n:2 * n]
    (q_ref, mask_ref, nmask_ref, nk_ref, nv_ref, bss_ref, oc_ref, ow_ref, gt_ref,
     o_ref, m_ref, l_ref, acc_ref, newk_ref, newv_ref) = refs[2 * n:]
    p = pl.program_id(1)
    n_steps = pl.num_programs(1)
    q = q_ref[...]
    lane = lax.broadcasted_iota(jnp.int32, (ROWS_S, PAGE_SIZE), 1)
    first_half = lane < SEL_BLOCK

    @pl.when(p == 0)
    def _():
        m_ref[...] = jnp.full(m_ref.shape, NEG, F32)
        l_ref[...] = jnp.zeros(l_ref.shape, F32)
        acc_ref[...] = jnp.zeros(acc_ref.shape, F32)

    def scores(qk, mk, bias, extra_valid=None):
        s = qk * SCALE + bias
        mk = jnp.where(first_half[:KT], mk[:, 0:1], mk[:, 1:2])
        on = jnp.concatenate([mk] * NSA_GROUP, axis=0) > 0.0
        if extra_valid is not None:
            on = on & extra_valid
        return jnp.where(on, s, NEG)

    def update(s_list, v_list, paged):
        m_old = m_ref[...]
        m_new = m_old
        for s in s_list:
            m_new = jnp.maximum(m_new, jnp.max(s, axis=1, keepdims=True))
        alpha = jnp.exp(m_old - m_new)
        l = alpha * l_ref[...]
        acc = alpha * acc_ref[...]
        for s, v in zip(s_list, v_list):
            e = jnp.exp(s - m_new)
            l = l + jnp.sum(e, axis=1, keepdims=True)
            pv = _dot_nt if paged else _dot
            acc = acc + pv(e.astype(BF16), v.astype(BF16))
        m_ref[...] = m_new
        l_ref[...] = l
        acc_ref[...] = acc

    s_list = []
    for c in range(n):
        ks = pl.multiple_of((p * n + c) * PAGE_SIZE, PAGE_SIZE)
        s_list.append(scores(_dot(q, kp[c][...].astype(BF16)), mask_ref[c], bss_ref[:, pl.ds(ks, PAGE_SIZE)]))
    update(s_list, [r[...] for r in vp], True)

    @pl.when(p == n_steps - 1)
    def _():
        for src, dst in ((nk_ref, newk_ref), (nv_ref, newv_ref)):
            dst[...] = jnp.zeros(dst.shape, F32)
            dst[0:4, :] = src[...]
        t_row = lax.broadcasted_iota(jnp.int32, (ROWS_S, PAGE_SIZE), 0) % 4
        s_new = scores(_dot_nt(q, newk_ref[...].astype(BF16)), nmask_ref[...],
                       bss_ref[:, past:past + PAGE_SIZE], lane <= t_row)
        update([s_new], [newv_ref[...]], False)
        kv_of_row = (lax.broadcasted_iota(jnp.int32, (ROWS_S, 1), 0) % KT) // 4
        o_s = _own_kv_lanes(acc_ref[...] / jnp.maximum(l_ref[...], 1e-30), kv_of_row)
        gate = jax.nn.sigmoid(gt_ref[...])
        o_ref[...] = gate[:, 0:1] * oc_ref[...] + gate[:, 1:2] * o_s + gate[:, 2:3] * ow_ref[...]


def _nsa_sample_sel(page_table, cache_k, cache_v, qn, mask_pages, new_k, new_v, bss, o_c, o_w, gates, *, past):
    nb, n_pages = page_table.shape
    n = PAGES_PER_STEP
    n_steps = n_pages // n
    t_new = new_k.shape[1]
    ck2, cv2 = cache_k, cache_v
    page_of = lambda b, p, c, pt: pt[b, p * n + c]
    per_b = lambda shape: pl.BlockSpec((None,) + shape, lambda b, p, pt: (b,) + (0,) * len(shape))
    grid_spec = pltpu.PrefetchScalarGridSpec(
        num_scalar_prefetch=1,
        grid=(nb, n_steps),
        in_specs=(_page_specs(n, KV_W, page_of) + _page_specs(n, KV_W, page_of)
                  + [per_b((ROWS_S, KV_W)),
                     pl.BlockSpec((None, n, KT, 2), lambda b, p, pt: (b, p, 0, 0)),
                     pl.BlockSpec((None, None, KT, 2), lambda b, p, pt: (b, n_pages, 0, 0)),
                     per_b((t_new, KV_W)), per_b((t_new, KV_W)),
                     pl.BlockSpec(bss.shape, lambda b, p, pt: (0, 0)),
                     per_b((ROWS_S, HEAD_DIM)), per_b((ROWS_S, HEAD_DIM)), per_b((ROWS_S, 3))]),
        out_specs=per_b((ROWS_S, HEAD_DIM)),
        scratch_shapes=[pltpu.VMEM((ROWS_S, 1), F32), pltpu.VMEM((ROWS_S, 1), F32),
                        pltpu.VMEM((ROWS_S, KV_W), F32),
                        pltpu.VMEM((PAGE_SIZE, KV_W), F32), pltpu.VMEM((PAGE_SIZE, KV_W), F32)])
    return pl.pallas_call(
        functools.partial(_nsa_sample_sel_kernel, past=past),
        grid_spec=grid_spec,
        out_shape=jax.ShapeDtypeStruct((nb, ROWS_S, HEAD_DIM), F32),
        compiler_params=_cparams(("parallel", "arbitrary")),
        name="nsa_sample_sel",
    )(page_table, *([ck2] * n), *([cv2] * n), qn, mask_pages, mask_pages, new_k, new_v, bss, o_c, o_w, gates)


SB_AHEAD = 2


def _sb_sample_kernel(pt_ref, q_ref, nk_ref, nv_ref, ck_hbm, cv_hbm, o_ref,
                      ls_ref, acc_ref, newk_ref, newv_ref, top_ref,
                      kbuf, vbuf, ksem, vsem, kx, vx, xsem):
    b = pl.program_id(0)
    nb = pl.num_programs(0)
    n_pages = pt_ref.shape[1]
    slot = b % 2
    q = q_ref[...]
    tri = _sb_tri()

    def ahead(seq, sl, j):
        page = pt_ref[seq, n_pages - 1 - j]
        return (pltpu.make_async_copy(ck_hbm.at[page], kbuf.at[sl, j], ksem.at[sl, j]),
                pltpu.make_async_copy(cv_hbm.at[page], vbuf.at[sl, j], vsem.at[sl, j]))

    def start_ahead(seq, sl):
        for j in range(SB_AHEAD):
            for cp in ahead(seq, sl, j):
                cp.start()

    @pl.when(b == 0)
    def _():
        start_ahead(0, 0)

    @pl.when(b + 1 < nb)
    def _():
        start_ahead(b + 1, 1 - slot)

    def block(keys, vals, valid):
        paged = valid is None
        qk = _dot if paged else _dot_nt
        pv = _dot_nt if paged else _dot
        z = qk(q, keys.astype(BF16)) * SCALE
        ln = _log_sigmoid_neg(z)
        ln_m = ln if valid is None else jnp.where(valid, ln, 0.0)
        st = _split_dot(ln_m, tri)
        a = jnp.exp(z + ln + st[:, :PAGE_SIZE] + ls_ref[...])
        if valid is not None:
            a = jnp.where(valid, a, 0.0)
        acc_ref[...] += pv(a.astype(BF16), vals.astype(BF16))
        ls_ref[...] += st[:, PAGE_SIZE:]
        top_ref[0] = jnp.max(ls_ref[...])

    ls_ref[...] = jnp.zeros(ls_ref.shape, F32)
    acc_ref[...] = jnp.zeros(acc_ref.shape, F32)
    for src, dst in ((nk_ref, newk_ref), (nv_ref, newv_ref)):
        dst[...] = jnp.zeros(dst.shape, F32)
        dst[0:4, :] = src[...]
    lane = lax.broadcasted_iota(jnp.int32, (ROWS_S, PAGE_SIZE), 1)
    t_row = lax.broadcasted_iota(jnp.int32, (ROWS_S, PAGE_SIZE), 0) % 4
    block(newk_ref[...], newv_ref[...], lane < t_row)

    for j in range(SB_AHEAD):
        for cp in ahead(b, slot, j):
            cp.wait()

        @pl.when(top_ref[0] > EXP_ZERO)
        def _():
            block(kbuf[slot, j], vbuf[slot, j], None)

    def more(carry):
        j, top = carry
        return (j < n_pages) & (top > EXP_ZERO)

    def fetch(carry):
        j, _ = carry
        page = pt_ref[b, n_pages - 1 - j]
        ck = pltpu.make_async_copy(ck_hbm.at[page], kx, xsem.at[0])
        cv = pltpu.make_async_copy(cv_hbm.at[page], vx, xsem.at[1])
        ck.start()
        cv.start()
        ck.wait()
        cv.wait()
        block(kx[...], vx[...], None)
        return j + 1, top_ref[0]

    lax.while_loop(more, fetch, (SB_AHEAD, top_ref[0]))

    acc = acc_ref[...]
    head_of_row = lax.broadcasted_iota(jnp.int32, (ROWS_S, 1), 0) // 4
    o = jnp.zeros((ROWS_S, HEAD_DIM), F32)
    for h in range(SB_HEADS):
        o = jnp.where(head_of_row == h, acc[:, h * HEAD_DIM:(h + 1) * HEAD_DIM], o)
    o_ref[...] = o


def _sb_sample(page_table, cache_k, cache_v, qs, new_k, new_v):
    nb, n_pages = page_table.shape
    assert n_pages >= SB_AHEAD
    t_new = new_k.shape[1]
    per_b = lambda shape: pl.BlockSpec((None,) + shape, lambda b, pt: (b,) + (0,) * len(shape))
    page = (SB_W, PAGE_SIZE)
    grid_spec = pltpu.PrefetchScalarGridSpec(
        num_scalar_prefetch=1,
        grid=(nb,),
        in_specs=[per_b((ROWS_S, SB_W)), per_b((t_new, SB_W)), per_b((t_new, SB_W)),
                  pl.BlockSpec(memory_space=pl.ANY), pl.BlockSpec(memory_space=pl.ANY)],
        out_specs=per_b((ROWS_S, HEAD_DIM)),
        scratch_shapes=[pltpu.VMEM((ROWS_S, PAGE_SIZE), F32), pltpu.VMEM((ROWS_S, SB_W), F32),
                        pltpu.VMEM((PAGE_SIZE, SB_W), F32), pltpu.VMEM((PAGE_SIZE, SB_W), F32),
                        pltpu.SMEM((1,), F32),
                        pltpu.VMEM((2, SB_AHEAD) + page, F32), pltpu.VMEM((2, SB_AHEAD) + page, F32),
                        pltpu.SemaphoreType.DMA((2, SB_AHEAD)), pltpu.SemaphoreType.DMA((2, SB_AHEAD)),
                        pltpu.VMEM(page, F32), pltpu.VMEM(page, F32), pltpu.SemaphoreType.DMA((2,))])
    return pl.pallas_call(
        _sb_sample_kernel,
        grid_spec=grid_spec,
        out_shape=jax.ShapeDtypeStruct((nb, ROWS_S, HEAD_DIM), F32),
        compiler_params=_cparams(("arbitrary",)),
        name="sb_sample",
    )(page_table, qs, new_k, new_v, cache_k, cache_v)


def _sample_group(xs, f32, page_table, caches, states, rel_bias, cmp_w_k, cmp_w_v, past):
    cache_sb_k, cache_sb_v, cache_cmp_k, cache_cmp_v, cache_sel_k, cache_sel_v = caches
    win_k, win_v = states
    nb, n_pages = page_table.shape
    t_new = xs.shape[0] // nb
    rs = lambda a, w: a.reshape(nb, t_new, w)

    sq = f32["sq"].reshape(nb, t_new, SB_HEADS, HEAD_DIM).transpose(0, 2, 1, 3)
    qs = jnp.einsum("bhtd,hk->bhtkd", sq, jnp.eye(SB_HEADS, dtype=F32))
    qs = qs.reshape(nb, ROWS_S, SB_W).astype(BF16)
    nq = f32["nq"].reshape(nb, t_new, NSA_KV_HEADS, NSA_GROUP, HEAD_DIM).transpose(0, 3, 2, 1, 4)
    qn = jnp.einsum("bgktd,kj->bgktjd", nq, jnp.eye(NSA_KV_HEADS, dtype=F32))
    qn = qn.reshape(nb, ROWS_S, KV_W).astype(BF16)

    o_sb = _sb_sample(page_table, _paged(cache_sb_k), _paged(cache_sb_v), qs,
                      rs(f32["sk"], SB_W), rs(f32["sv"], SB_W))
    o_sb = o_sb.reshape(nb, SB_HEADS, t_new, HEAD_DIM).transpose(0, 2, 1, 3).reshape(nb * t_new, SB_W)

    n_cmp = past // CMP_STRIDE
    n_sel = -(-(past + t_new) // SEL_BLOCK)
    n_sel_pad = -(-2 * (n_pages + 1) // 8) * 8
    w_buf = win_k.shape[1]
    w_pad = w_buf + LANES
    nn = np.arange(n_cmp)[None, :]
    jj = np.arange(n_sel_pad)[:, None]
    a_t = jnp.asarray(((nn >= SEL_PER_CMP * jj - 1) & (nn <= SEL_PER_CMP * jj + SEL_PER_CMP - 1)
                       & (jj < n_sel)).astype(np.float32), BF16)
    r = np.arange(ROWS_S)
    head_of_row = ((r % KT) // 4) * NSA_GROUP + r // KT
    rbr = rel_bias.T[head_of_row]
    bsc, bsw, bss = _bias_sample(rbr, past, w_buf, n_cmp, w_pad)

    o_c, o_w, sel = _nsa_sample_cmp(
        page_table, _paged(cache_cmp_k), _paged(cache_cmp_v), qn, rs(f32["ck"], KV_W), rs(f32["cv"], KV_W),
        _paged(win_k), _paged(win_v), rs(f32["wk"], KV_W), rs(f32["wv"], KV_W),
        cmp_w_k, cmp_w_v, a_t, bsc, bsw, past=past)

    mask_pages = sel[:, :2 * (n_pages + 1)].reshape(nb, n_pages + 1, 2, KT).transpose(0, 1, 3, 2)
    gates = f32["gt"][:, :GATE_W].reshape(nb, t_new, 3, NSA_KV_HEADS, NSA_GROUP)
    gates = gates.transpose(0, 4, 3, 1, 2).reshape(nb, ROWS_S, 3)
    o_nsa = _nsa_sample_sel(page_table, _paged(cache_sel_k), _paged(cache_sel_v), qn, mask_pages,
                            rs(f32["sk2"], KV_W), rs(f32["sv2"], KV_W), bss, o_c, o_w, gates, past=past)
    o_nsa = o_nsa.reshape(nb, NSA_GROUP, NSA_KV_HEADS, t_new, HEAD_DIM).transpose(0, 3, 2, 1, 4)
    return o_sb, o_nsa.reshape(nb * t_new, NSA_W)


def kernel(x_prompt, x_sample, cache_sb_k, cache_sb_v, cache_cmp_k, cache_cmp_v, cache_sel_k, cache_sel_v, state_win_k, state_win_v, page_table, ffn1_pre_g, ffn1_post_g, ffn1_w_gate, ffn1_w_up, ffn1_w_down, mix_pre_g, mix_post_g, w_in, cmp_pos_k, cmp_pos_v, w_cmp_k1, w_cmp_k2, w_cmp_v1, w_cmp_v2, sb_out_g, nsa_out_g, w_o, rel_bias, ffn2_pre_g, ffn2_post_g, ffn2_w_gate, ffn2_w_up, ffn2_w_down):
    depth = w_in.shape[0]
    assert depth == 1, "caches are updated for a single layer"
    nb_p, s_len, d_model = x_prompt.shape
    nb_s, t_new, _ = x_sample.shape
    assert nb_p == 1 and t_new == 4 and s_len % (CMP_CHUNK_SPAN) == 0
    n_pages = page_table.shape[1]
    past = n_pages * PAGE_SIZE
    assert n_pages % PAGES_PER_STEP == 0 and past % SEL_BLOCK == 0
    w_keep = min(WINDOW, s_len)
    tm_p = 512
    tm_s = min(512, nb_s * t_new)
    l = 0

    ffn1 = (ffn1_pre_g[l], ffn1_post_g[l], ffn1_w_gate[l].astype(BF16), ffn1_w_up[l].astype(BF16),
            ffn1_w_down[l].astype(BF16))
    ffn2 = (ffn2_pre_g[l], ffn2_post_g[l], ffn2_w_gate[l].astype(BF16), ffn2_w_up[l].astype(BF16),
            ffn2_w_down[l].astype(BF16))
    w_in_pad = _pad_w_in(w_in[l])
    w_o_b = w_o[l].astype(BF16)
    cmp_w_k = _compress_weights(w_cmp_k1[l], w_cmp_k2[l], cmp_pos_k[l])
    cmp_w_v = _compress_weights(w_cmp_v1[l], w_cmp_v2[l], cmp_pos_v[l])

    xp = _ffn(x_prompt[0], *ffn1, tm=tm_p)
    pf, pb = _in_proj(xp, mix_pre_g[l], w_in_pad, tm=tm_p)
    o_sb = _sb_prompt(pb["sq"], pb["sk"], pb["sv"])
    o_nsa = _nsa_prompt_group(rel_bias, pf, pb, cmp_w_k, cmp_w_v)
    xp = _out_proj(xp, o_sb, o_nsa, sb_out_g[l], nsa_out_g[l], w_o_b, mix_post_g[l], tm=tm_p)
    xp = _ffn(xp, *ffn2, tm=tm_p)

    xs = _ffn(x_sample.reshape(nb_s * t_new, d_model), *ffn1, tm=tm_s)
    sf, _ = _in_proj(xs, mix_pre_g[l], w_in_pad, tm=tm_s)
    caches = (cache_sb_k[l], cache_sb_v[l], cache_cmp_k[l], cache_cmp_v[l], cache_sel_k[l], cache_sel_v[l])
    win = (state_win_k[l], state_win_v[l])
    o_sb_s, o_nsa_s = _sample_group(xs, sf, page_table, caches, win, rel_bias, cmp_w_k, cmp_w_v, past)
    xs = _out_proj(xs, o_sb_s, o_nsa_s, sb_out_g[l], nsa_out_g[l], w_o_b, mix_post_g[l], tm=tm_s)
    xs = _ffn(xs, *ffn2, tm=tm_s)

    sbh = lambda a, n: a.reshape(1, -1, n, SB_HEADS, HEAD_DIM)
    kvh = lambda a, n: a.reshape(1, -1, n, NSA_KV_HEADS, HEAD_DIM)
    prompt_kv = [kvh(pf[name], s_len) for name in ("ck", "cv", "sk2", "sv2")]
    sample_kv = [kvh(sf[name], t_new) for name in ("ck", "cv", "sk2", "sv2")]
    new_wk = jnp.concatenate([state_win_k[l].reshape(nb_s, -1, KV_W), sf["wk"].reshape(nb_s, t_new, KV_W)],
                             axis=1)[:, t_new:]
    new_wv = jnp.concatenate([state_win_v[l].reshape(nb_s, -1, KV_W), sf["wv"].reshape(nb_s, t_new, KV_W)],
                             axis=1)[:, t_new:]
    return (xp[None], xs.reshape(nb_s, t_new, d_model),
            sbh(pf["sk"], s_len), sbh(pf["sv"], s_len), *prompt_kv,
            kvh(pf["wk"][s_len - w_keep:], w_keep), kvh(pf["wv"][s_len - w_keep:], w_keep),
            sbh(sf["sk"], t_new), sbh(sf["sv"], t_new), *sample_kv,
            kvh(new_wk, new_wk.shape[1]), kvh(new_wv, new_wv.shape[1]))
```
